```python
import math
import jax, jax.numpy as jnp
from jax import lax
import numpy as np

D_MODEL = 2048
BATCH = 2
SEQ = 16384
DEPTH = 1

CHUNK = 64
EPS = 1e-6
N_MOD = 9
D_FF = 5632
SSD_D_INNER = 2048
SSD_HEAD_DIM = 64
SSD_HEADS = SSD_D_INNER // SSD_HEAD_DIM
SSD_GROUPS = 8
SSD_STATE = 128
SSD_CONV = 4
SSD_XBC = SSD_D_INNER + 2 * SSD_GROUPS * SSD_STATE
ATT_HEADS = 16
ATT_KV_HEADS = 4
ATT_HEAD_DIM = 128
ATT_WIDTH = ATT_HEADS * ATT_HEAD_DIM
ATT_KV_WIDTH = ATT_KV_HEADS * ATT_HEAD_DIM
IDX_HEADS = 16
IDX_HEAD_DIM = 64
TOPK_MAX = 256
Q_BLOCK = 128
ALIBI_MAX_BIAS = 8.0
N_BRANCH = 2
IN_SPLITS = (SSD_D_INNER, SSD_XBC, SSD_HEADS, ATT_WIDTH, ATT_KV_WIDTH, ATT_KV_WIDTH,
             IDX_HEADS * IDX_HEAD_DIM, IDX_HEAD_DIM, IDX_HEADS, N_BRANCH * D_MODEL)
D_IN_PROJ = sum(IN_SPLITS)

kernel_name = 'hybrid_ssd_dsa_macaron_block'


def _rms(x, w):
    xf = x.astype(jnp.float32)
    y = xf * lax.rsqrt(jnp.mean(xf * xf, axis=-1, keepdims=True) + EPS)
    return (y * w.astype(jnp.float32)).astype(x.dtype)


def _modulate(x, shift, scale):
    return x * (1 + scale[:, None, :]) + shift[:, None, :]


def _swiglu(x, w1, w3, w2):
    return (jax.nn.silu(x @ w1) * (x @ w3)) @ w2


def _split_cols(u):
    outs, off = [], 0
    for n in IN_SPLITS:
        outs.append(u[..., off:off + n])
        off += n
    return outs


def _causal_dwconv(x, w, b):
    k, ch = w.shape
    y = lax.conv_general_dilated(x, w[:, None, :].astype(x.dtype), window_strides=(1,),
                                 padding=[(k - 1, 0)], dimension_numbers=('NWC', 'WIO', 'NWC'),
                                 feature_group_count=ch)
    return y + b


def _ssd_scan(xdt, dta, bm, cm):
    b, L, G, R, P = xdt.shape
    N = bm.shape[-1]
    nc = L // CHUNK

    def to_chunks(a):
        return jnp.moveaxis(a.reshape(b, nc, CHUNK, *a.shape[2:]), 1, 0)

    causal = jnp.tril(jnp.ones((CHUNK, CHUNK), bool))[None, :, :, None, None]

    def step(state, inp):
        xc, ac, bc, cc = inp
        acum = jnp.cumsum(ac, axis=1)
        seg = acum[:, :, None] - acum[:, None, :]
        decay = jnp.exp(jnp.where(causal, seg, -jnp.inf))
        cb = jnp.einsum('btgn,bsgn->btsg', cc, bc)
        y_diag = jnp.einsum('btsg,btsgr,bsgrp->btgrp', cb, decay, xc)
        y_off = jnp.einsum('btgn,bgrpn,btgr->btgrp', cc, state, jnp.exp(acum))
        a_last = acum[:, -1]
        w_s = jnp.exp(a_last[:, None] - acum)
        new_state = state * jnp.exp(a_last)[..., None, None] + jnp.einsum(
            'bsgn,bsgr,bsgrp->bgrpn', bc, w_s, xc)
        return new_state.astype(jnp.float32), (y_diag + y_off).astype(jnp.float32)

    state0 = jnp.zeros((b, G, R, P, N), jnp.float32)
    xs = (to_chunks(xdt), to_chunks(dta), to_chunks(bm), to_chunks(cm))
    _, ys = lax.scan(step, state0, xs)
    return jnp.moveaxis(ys, 0, 1).reshape(b, L, G, R, P).astype(xdt.dtype)


def _ssd_branch(z, xbc, dt_raw, conv_w, conv_b, dt_bias, a_log, d_skip, norm_w):
    b, L, _ = z.shape
    R = SSD_HEADS // SSD_GROUPS
    xbc = jax.nn.silu(_causal_dwconv(xbc, conv_w, conv_b))
    xs = xbc[..., :SSD_D_INNER].reshape(b, L, SSD_GROUPS, R, SSD_HEAD_DIM)
    bm = xbc[..., SSD_D_INNER:SSD_D_INNER + SSD_GROUPS * SSD_STATE].reshape(b, L, SSD_GROUPS, SSD_STATE)
    cm = xbc[..., SSD_D_INNER + SSD_GROUPS * SSD_STATE:].reshape(b, L, SSD_GROUPS, SSD_STATE)
    dt = jax.nn.softplus((dt_raw + dt_bias).astype(jnp.float32)).reshape(b, L, SSD_GROUPS, R)
    a = -jnp.exp(a_log.astype(jnp.float32)).reshape(SSD_GROUPS, R)
    y = _ssd_scan(xs * dt[..., None].astype(xs.dtype), dt * a, bm, cm)
    y = y + d_skip.reshape(SSD_GROUPS, R)[:, :, None] * xs
    y = y.reshape(b, L, SSD_D_INNER) * jax.nn.silu(z)
    y = _rms(y.reshape(b, L, SSD_GROUPS, SSD_D_INNER // SSD_GROUPS),
             norm_w.reshape(SSD_GROUPS, SSD_D_INNER // SSD_GROUPS))
    return y.reshape(b, L, SSD_D_INNER)


def _alibi_slopes():
    h = jnp.arange(1, ATT_HEADS + 1, dtype=jnp.float32)
    return jnp.exp2(-ALIBI_MAX_BIAS * h / ATT_HEADS)


def _dsa_branch(q, k, v, q_idx, k_idx, w_idx, positions, q_norm_w, k_norm_w):
    b, L, _ = q.shape
    rq = ATT_HEADS // ATT_KV_HEADS
    top_k = min(TOPK_MAX, L // 4)
    q = _rms(q.reshape(b, L, ATT_HEADS, ATT_HEAD_DIM), q_norm_w).reshape(b, L, ATT_KV_HEADS, rq, ATT_HEAD_DIM)
    k = _rms(k.reshape(b, L, ATT_KV_HEADS, ATT_HEAD_DIM), k_norm_w)
    v = v.reshape(b, L, ATT_KV_HEADS, ATT_HEAD_DIM)
    q_idx = q_idx.reshape(b, L, IDX_HEADS, IDX_HEAD_DIM)
    key_chunk = positions // CHUNK
    slopes = _alibi_slopes().reshape(ATT_KV_HEADS, rq)
    idx_scale = (IDX_HEAD_DIM * IDX_HEADS) ** -0.5
    att_scale = ATT_HEAD_DIM ** -0.5
    nblk = L // Q_BLOCK
    gather = jax.vmap(lambda arr, ix: arr[ix])

    def blocks(a):
        return jnp.moveaxis(a.reshape(b, nblk, Q_BLOCK, *a.shape[2:]), 1, 0)

    def one_block(inp):
        qb, qib, wb, pb = inp
        qchunk = pb // CHUNK
        rel = jax.nn.relu(jnp.einsum('bthd,bsd->bths', qib, k_idx))
        score = jnp.einsum('bths,bth->bts', rel, wb).astype(jnp.float32) * idx_scale
        admissible = key_chunk[:, None, :] <= qchunk[:, :, None]
        score = jnp.where(admissible, score, -jnp.inf)
        _, sel = lax.top_k(score, top_k)
        k_sel = gather(k, sel)
        v_sel = gather(v, sel)
        p_sel = gather(positions, sel)
        valid = (p_sel // CHUNK) <= qchunk[..., None]
        dist = jnp.abs(pb[..., None] - p_sel).astype(jnp.float32)
        logits = jnp.einsum('btgrd,btkgd->btgrk', qb, k_sel).astype(jnp.float32) * att_scale
        logits = logits - slopes[None, None, :, :, None] * dist[:, :, None, None, :]
        logits = jnp.where(valid[:, :, None, None, :], logits, -jnp.inf)
        probs = jax.nn.softmax(logits, axis=-1).astype(v.dtype)
        return jnp.einsum('btgrk,btkgd->btgrd', probs, v_sel)

    out = lax.map(one_block, (blocks(q), blocks(q_idx), blocks(w_idx), blocks(positions)))
    return jnp.moveaxis(out, 0, 1).reshape(b, L, ATT_WIDTH)


def _mixer(hn, positions, w_in, conv_w, conv_b, dt_bias, a_log, d_skip, ssd_norm,
           q_norm_w, k_norm_w, w_br_ssd, w_br_att, w_out):
    b, L, _ = hn.shape
    u = hn @ w_in
    z, xbc, dt_raw, q, k, v, q_idx, k_idx, w_idx, gate_logits = _split_cols(u)
    y_ssd = _ssd_branch(z, xbc, dt_raw, conv_w, conv_b, dt_bias, a_log, d_skip, ssd_norm)
    y_att = _dsa_branch(q, k, v, q_idx, k_idx, w_idx, positions, q_norm_w, k_norm_w)
    g = jax.nn.sigmoid(gate_logits).reshape(b, L, N_BRANCH, D_MODEL)
    merged = g[:, :, 0] * (y_ssd @ w_br_ssd) + g[:, :, 1] * (y_att @ w_br_att)
    return merged @ w_out


def setup_inputs(seed: int = 0) -> dict:
    key = jax.random.key(seed)
    ks = jax.random.split(key, 32)
    f32 = jnp.float32

    def nrm(k, shape, fan_in, gain=1.0):
        return jax.random.normal(k, shape, f32) * (gain * fan_in ** -0.5)

    def gain(k, shape):
        return 1.0 + 0.02 * jax.random.normal(k, shape, f32)

    x = jax.random.normal(ks[0], (BATCH, SEQ, D_MODEL), f32)
    c = jax.random.normal(ks[1], (BATCH, D_MODEL), f32)
    start = jax.random.randint(ks[2], (BATCH, 1), 0, 8, jnp.int32) * CHUNK
    positions = (start + jnp.arange(SEQ, dtype=jnp.int32)[None, :]).astype(jnp.int32)
    dt0 = jnp.exp(jax.random.uniform(ks[3], (DEPTH, SSD_HEADS), f32, math.log(1e-3), math.log(1e-1)))
    return {
        'x': x,
        'c': c,
        'positions': positions,
        'w_ada': nrm(ks[4], (DEPTH, D_MODEL, N_MOD * D_MODEL), D_MODEL, 0.5),
        'b_ada': 0.01 * jax.random.normal(ks[5], (DEPTH, N_MOD * D_MODEL), f32),
        'norm_ffn1': gain(ks[6], (DEPTH, D_MODEL)),
        'ffn1_w1': nrm(ks[7], (DEPTH, D_MODEL, D_FF), D_MODEL),
        'ffn1_w3': nrm(ks[8], (DEPTH, D_MODEL, D_FF), D_MODEL),
        'ffn1_w2': nrm(ks[9], (DEPTH, D_FF, D_MODEL), D_FF),
        'norm_mix': gain(ks[10], (DEPTH, D_MODEL)),
        'w_in': nrm(ks[11], (DEPTH, D_MODEL, D_IN_PROJ), D_MODEL),
        'ssd_conv_w': nrm(ks[12], (DEPTH, SSD_CONV, SSD_XBC), SSD_CONV),
        'ssd_conv_b': 0.01 * jax.random.normal(ks[13], (DEPTH, SSD_XBC), f32),
        'ssd_dt_bias': dt0 + jnp.log(-jnp.expm1(-dt0)),
        'ssd_a_log': jnp.log(jax.random.uniform(ks[14], (DEPTH, SSD_HEADS), f32, 1.0, 16.0)),
        'ssd_d': gain(ks[15], (DEPTH, SSD_HEADS)),
        'ssd_norm': gain(ks[16], (DEPTH, SSD_D_INNER)),
        'q_norm': gain(ks[17], (DEPTH, ATT_HEAD_DIM)),
        'k_norm': gain(ks[18], (DEPTH, ATT_HEAD_DIM)),
        'w_br_ssd': nrm(ks[19], (DEPTH, SSD_D_INNER, D_MODEL), SSD_D_INNER),
        'w_br_att': nrm(ks[20], (DEPTH, ATT_WIDTH, D_MODEL), ATT_WIDTH),
        'w_out': nrm(ks[21], (DEPTH, D_MODEL, D_MODEL), D_MODEL),
        'norm_ffn2': gain(ks[22], (DEPTH, D_MODEL)),
        'ffn2_w1': nrm(ks[23], (DEPTH, D_MODEL, D_FF), D_MODEL),
        'ffn2_w3': nrm(ks[24], (DEPTH, D_MODEL, D_FF), D_MODEL),
        'ffn2_w2': nrm(ks[25], (DEPTH, D_FF, D_MODEL), D_FF),
    }


def reference(x, c, positions, w_ada, b_ada, norm_ffn1, ffn1_w1, ffn1_w3, ffn1_w2, norm_mix, w_in,
              ssd_conv_w, ssd_conv_b, ssd_dt_bias, ssd_a_log, ssd_d, ssd_norm, q_norm, k_norm,
              w_br_ssd, w_br_att, w_out, norm_ffn2, ffn2_w1, ffn2_w3, ffn2_w2):
    h = x
    c_act = jax.nn.silu(c)
    for l in range(DEPTH):
        mod = c_act @ w_ada[l] + b_ada[l]
        sh1, sc1, g1, sh2, sc2, g2, sh3, sc3, g3 = jnp.split(mod, N_MOD, axis=-1)
        hn = _modulate(_rms(h, norm_ffn1[l]), sh1, sc1)
        h = h + 0.5 * g1[:, None, :] * _swiglu(hn, ffn1_w1[l], ffn1_w3[l], ffn1_w2[l])
        hn = _modulate(_rms(h, norm_mix[l]), sh2, sc2)
        h = h + g2[:, None, :] * _mixer(hn, positions, w_in[l], ssd_conv_w[l], ssd_conv_b[l],
                                        ssd_dt_bias[l], ssd_a_log[l], ssd_d[l], ssd_norm[l],
                                        q_norm[l], k_norm[l], w_br_ssd[l], w_br_att[l], w_out[l])
        hn = _modulate(_rms(h, norm_ffn2[l]), sh3, sc3)
        h = h + 0.5 * g3[:, None, :] * _swiglu(hn, ffn2_w1[l], ffn2_w3[l], ffn2_w2[l])
    return h
```

```python
import functools
import math

import numpy as np
import jax
import jax.numpy as jnp
from jax import lax
from jax.experimental import pallas as pl
from jax.experimental.pallas import tpu as pltpu

F32 = jnp.float32
BF16 = jnp.bfloat16

CHUNK = 64
EPS = 1e-6
N_MOD = 9
SSD_D_INNER = 2048
SSD_HEAD_DIM = 64
SSD_GROUPS = 8
SSD_HEADS = SSD_D_INNER // SSD_HEAD_DIM
SSD_R = SSD_HEADS // SSD_GROUPS
SSD_GW = SSD_D_INNER // SSD_GROUPS
SSD_STATE = 128
SSD_CONV = 4
ATT_HEADS = 16
ATT_KV_HEADS = 4
ATT_RQ = ATT_HEADS // ATT_KV_HEADS
ATT_HEAD_DIM = 128
ATT_WIDTH = ATT_HEADS * ATT_HEAD_DIM
ATT_KV_WIDTH = ATT_KV_HEADS * ATT_HEAD_DIM
IDX_HEADS = 16
IDX_HEAD_DIM = 64
TOPK_MAX = 256
ALIBI_MAX_BIAS = 8.0
LOG2E = 1.4426950408889634
NEG_BIG = -1e30

LANES = 128
QK_AUG = 256
VMEM_LIMIT = 56 * 1024 * 1024

OFF_Z = 0
OFF_XBC = OFF_Z + SSD_D_INNER
OFF_Q = OFF_XBC + SSD_D_INNER + 2 * SSD_GROUPS * SSD_STATE
OFF_K = OFF_Q + ATT_WIDTH
OFF_V = OFF_K + ATT_KV_WIDTH
OFF_QI = OFF_V + ATT_KV_WIDTH
OFF_GATE = OFF_QI + IDX_HEADS * IDX_HEAD_DIM
SM_KIDX = 0
SM_WIDX = IDX_HEAD_DIM
SM_DT = SM_WIDX + IDX_HEADS

FFN_TM, FFN_TF = 512, 512
INP_TM, INP_TN = 512, 512
SSD_Q = 256
PREP_T = 256
IDX_TQ, IDX_KC, IDX_RB, IDX_SB = 128, 256, 16, 64
ATT_TQ, ATT_TK = 256, 512
MRG_TM, MRG_TN = 512, 512


def _cparams(sem):
    return pltpu.CompilerParams(dimension_semantics=sem, vmem_limit_bytes=VMEM_LIMIT)


def _sigmoid(x):
    return 1.0 / (1.0 + jnp.exp(-x))


def _rms_mod(x, nw, sh, sc):
    ms = jnp.mean(x * x, axis=-1, keepdims=True)
    return (x * lax.rsqrt(ms + EPS) * nw) * (1.0 + sc) + sh


def _ada_kernel(ct_ref, w_ref, b_ref, o_ref):
    ct = ct_ref[...]
    ca = ct * _sigmoid(ct)
    w = w_ref[...]
    rows = [jnp.sum(w * ca[:, b:b + 1], axis=0, keepdims=True) for b in range(ct.shape[1])]
    o_ref[...] = jnp.concatenate(rows, axis=0) + b_ref[...]


def _ada(c, w_ada, b_ada):
    bsz, d = c.shape
    n = w_ada.shape[1]
    tn = 1024 if n % 1024 == 0 else n
    return pl.pallas_call(
        _ada_kernel,
        out_shape=jax.ShapeDtypeStruct((bsz, n), F32),
        grid=(n // tn,),
        in_specs=[pl.BlockSpec((d, bsz), lambda j: (0, 0)),
                  pl.BlockSpec((d, tn), lambda j: (0, j)),
                  pl.BlockSpec((1, tn), lambda j: (0, j))],
        out_specs=pl.BlockSpec((bsz, tn), lambda j: (0, j)),
        compiler_params=_cparams(("arbitrary",)),
        name="ada",
    )(c.T, w_ada, b_ada.reshape(1, n))


def _ffn_kernel(x_ref, nw_ref, sh_ref, sc_ref, g_ref, w1_ref, w3_ref, w2_ref, o_ref, hn_ref):
    f = pl.program_id(2)

    @pl.when(f == 0)
    def _():
        hn_ref[...] = _rms_mod(x_ref[0], nw_ref[...], sh_ref[0], sc_ref[0]).astype(BF16)
        o_ref[0] = jnp.zeros(o_ref.shape[1:], F32)

    hn = hn_ref[...]
    a = jnp.dot(hn, w1_ref[...], preferred_element_type=F32)
    b = jnp.dot(hn, w3_ref[...], preferred_element_type=F32)
    g = (a * _sigmoid(a) * b).astype(BF16)
    o_ref[0] += jnp.dot(g, w2_ref[...], preferred_element_type=F32)

    @pl.when(f == pl.num_programs(2) - 1)
    def _():
        o_ref[0] = x_ref[0] + 0.5 * g_ref[0] * o_ref[0]


def _ffn(h, nw, sh, sc, gate, w1, w3, w2):
    bsz, L, d = h.shape
    ff = w1.shape[1]
    tm, tf = min(FFN_TM, L), min(FFN_TF, ff)
    vec = pl.BlockSpec((1, 1, d), lambda b, i, f: (b, 0, 0))
    return pl.pallas_call(
        _ffn_kernel,
        out_shape=jax.ShapeDtypeStruct((bsz, L, d), F32),
        grid=(bsz, L // tm, ff // tf),
        in_specs=[pl.BlockSpec((1, tm, d), lambda b, i, f: (b, i, 0)),
                  pl.BlockSpec((1, d), lambda b, i, f: (0, 0)),
                  vec, vec, vec,
                  pl.BlockSpec((d, tf), lambda b, i, f: (0, f)),
                  pl.BlockSpec((d, tf), lambda b, i, f: (0, f)),
                  pl.BlockSpec((tf, d), lambda b, i, f: (f, 0))],
        out_specs=pl.BlockSpec((1, tm, d), lambda b, i, f: (b, i, 0)),
        scratch_shapes=[pltpu.VMEM((tm, d), BF16)],
        compiler_params=_cparams(("parallel", "parallel", "arbitrary")),
        name="ffn",
    )(h, nw, sh, sc, gate, w1, w3, w2)


def _inproj_kernel(x_ref, nw_ref, sh_ref, sc_ref, w_ref, ws_ref, u_ref, s_ref, hn_ref):
    j = pl.program_id(2)

    @pl.when(j == 0)
    def _():
        hn = _rms_mod(x_ref[0], nw_ref[...], sh_ref[0], sc_ref[0]).astype(BF16)
        hn_ref[...] = hn
        s_ref[0] = jnp.dot(hn, ws_ref[...], preferred_element_type=F32)

    u_ref[0] = jnp.dot(hn_ref[...], w_ref[...], preferred_element_type=F32).astype(BF16)


def _inproj(h, nw, sh, sc, w_main, w_small):
    bsz, L, d = h.shape
    n = w_main.shape[1]
    tm, tn = min(INP_TM, L), INP_TN
    vec = pl.BlockSpec((1, 1, d), lambda b, i, j: (b, 0, 0))
    return pl.pallas_call(
        _inproj_kernel,
        out_shape=(jax.ShapeDtypeStruct((bsz, L, n), BF16),
                   jax.ShapeDtypeStruct((bsz, L, LANES), F32)),
        grid=(bsz, L // tm, n // tn),
        in_specs=[pl.BlockSpec((1, tm, d), lambda b, i, j: (b, i, 0)),
                  pl.BlockSpec((1, d), lambda b, i, j: (0, 0)),
                  vec, vec,
                  pl.BlockSpec((d, tn), lambda b, i, j: (0, j)),
                  pl.BlockSpec((d, LANES), lambda b, i, j: (0, 0))],
        out_specs=(pl.BlockSpec((1, tm, tn), lambda b, i, j: (b, i, j)),
                   pl.BlockSpec((1, tm, LANES), lambda b, i, j: (b, i, 0))),
        scratch_shapes=[pltpu.VMEM((tm, d), BF16)],
        compiler_params=_cparams(("parallel", "parallel", "arbitrary")),
        name="inproj",
    )(h, nw, sh, sc, w_main, w_small)


def _ssd_kernel(xs_ref, bm_ref, cm_ref, z_ref, dtr_ref, cwx_ref, cwb_ref, cwc_ref,
                cbx_ref, cbb_ref, cbc_ref, dtb_ref, alog_ref, dsk_ref, nw_ref,
                o_ref, ext_ref, state_ref):
    t = pl.program_id(2)
    q = xs_ref.shape[1]
    gw, ns = SSD_GW, SSD_STATE
    cw = gw + 2 * ns

    @pl.when(t == 0)
    def _():
        ext_ref[0:8, :] = jnp.zeros((8, cw), F32)
        state_ref[...] = jnp.zeros(state_ref.shape, F32)

    ext_ref[8:8 + q, 0:gw] = xs_ref[0].astype(F32)
    ext_ref[8:8 + q, gw:gw + ns] = bm_ref[0].astype(F32)
    ext_ref[8:8 + q, gw + ns:cw] = cm_ref[0].astype(F32)
    wts = jnp.concatenate([cwx_ref[...], cwb_ref[...], cwc_ref[...]], axis=1)
    acc = jnp.concatenate([cbx_ref[...], cbb_ref[...], cbc_ref[...]], axis=1)
    for j in range(SSD_CONV):
        acc = acc + wts[j:j + 1, :] * ext_ref[8 - (SSD_CONV - 1) + j:8 - (SSD_CONV - 1) + j + q, :]
    tail = ext_ref[q:q + 8, :]
    ext_ref[0:8, :] = tail
    xc = acc * _sigmoid(acc)
    xs = xc[:, 0:gw]
    bm = xc[:, gw:gw + ns].astype(BF16)
    cm = xc[:, gw + ns:cw].astype(BF16)
    xs_b = xs.astype(BF16)

    dtx = dtr_ref[0, 0] + dtb_ref[0]
    dt_row = jnp.maximum(dtx, 0.0) + jnp.log(1.0 + jnp.exp(-jnp.abs(dtx)))
    dta_row = dt_row * (-jnp.exp(alog_ref[0]))

    ti = lax.broadcasted_iota(jnp.int32, (q, q), 0)
    si = lax.broadcasted_iota(jnp.int32, (q, q), 1)
    tril = si <= ti
    eye = si == ti
    lane_head = lax.broadcasted_iota(jnp.int32, (1, gw), 1) // SSD_HEAD_DIM

    cb = lax.dot_general(cm, bm, (((1,), (1,)), ((), ())), preferred_element_type=F32)

    y = jnp.zeros((q, gw), F32)
    f_exp = jnp.zeros((q, gw), F32)
    f_w = jnp.zeros((q, gw), F32)
    e_dec = jnp.zeros((1, gw), F32)
    for r in range(SSD_R):
        dta_r = dta_row[r:r + 1, :]
        dt_r = dt_row[r:r + 1, :]
        acum_c = jnp.sum(jnp.where(tril, dta_r, 0.0), axis=1, keepdims=True)
        acum_r = jnp.sum(jnp.where(eye, acum_c, 0.0), axis=0, keepdims=True)
        dt_c = jnp.sum(jnp.where(eye, dt_r, 0.0), axis=1, keepdims=True)
        decay = jnp.exp(jnp.where(tril, acum_c - acum_r, -jnp.inf))
        m_r = (cb * decay * dt_r).astype(BF16)
        yd = jnp.dot(m_r, xs_b, preferred_element_type=F32)
        sel = lane_head == r
        y = y + jnp.where(sel, yd, 0.0)
        a_last = acum_r[:, q - 1:q]
        f_exp = f_exp + jnp.where(sel, jnp.exp(acum_c), 0.0)
        f_w = f_w + jnp.where(sel, dt_c * jnp.exp(a_last - acum_c), 0.0)
        e_dec = e_dec + jnp.where(sel, jnp.exp(a_last), 0.0)

    state = state_ref[...]
    y = y + jnp.dot(cm, state.astype(BF16), preferred_element_type=F32) * f_exp
    xw = (xs * f_w).astype(BF16)
    bm_t = jnp.transpose(xc[:, gw:gw + ns]).astype(BF16)
    state_ref[...] = state * e_dec + jnp.dot(bm_t, xw, preferred_element_type=F32)

    y = y + dsk_ref[...] * xs
    zf = z_ref[0].astype(F32)
    y = y * (zf * _sigmoid(zf))
    ms = jnp.mean(y * y, axis=-1, keepdims=True)
    o_ref[0] = (y * lax.rsqrt(ms + EPS) * nw_ref[...]).astype(BF16)


def _ssd(u, dt_rows, conv_w, conv_b, dt_bias, a_log, d_skip, norm_w):
    bsz, L, _ = u.shape
    q = min(SSD_Q, L)
    gw, ns, G = SSD_GW, SSD_STATE, SSD_GROUPS
    xb = OFF_XBC // gw
    bb = (OFF_XBC + SSD_D_INNER) // ns
    cb = bb + G
    cwb0 = SSD_D_INNER // ns
    return pl.pallas_call(
        _ssd_kernel,
        out_shape=jax.ShapeDtypeStruct((bsz, L, SSD_D_INNER), BF16),
        grid=(bsz, G, L // q),
        in_specs=[pl.BlockSpec((1, q, gw), lambda b, g, t: (b, t, xb + g)),
                  pl.BlockSpec((1, q, ns), lambda b, g, t: (b, t, bb + g)),
                  pl.BlockSpec((1, q, ns), lambda b, g, t: (b, t, cb + g)),
                  pl.BlockSpec((1, q, gw), lambda b, g, t: (b, t, g)),
                  pl.BlockSpec((1, 1, 8, q), lambda b, g, t: (b, g, 0, t)),
                  pl.BlockSpec((SSD_CONV, gw), lambda b, g, t: (0, g)),
                  pl.BlockSpec((SSD_CONV, ns), lambda b, g, t: (0, cwb0 + g)),
                  pl.BlockSpec((SSD_CONV, ns), lambda b, g, t: (0, cwb0 + G + g)),
                  pl.BlockSpec((1, gw), lambda b, g, t: (0, g)),
                  pl.BlockSpec((1, ns), lambda b, g, t: (0, cwb0 + g)),
                  pl.BlockSpec((1, ns), lambda b, g, t: (0, cwb0 + G + g)),
                  pl.BlockSpec((1, 8, 1), lambda b, g, t: (g, 0, 0)),
                  pl.BlockSpec((1, 8, 1), lambda b, g, t: (g, 0, 0)),
                  pl.BlockSpec((1, gw), lambda b, g, t: (0, g)),
                  pl.BlockSpec((1, gw), lambda b, g, t: (0, g))],
        out_specs=pl.BlockSpec((1, q, gw), lambda b, g, t: (b, t, g)),
        scratch_shapes=[pltpu.VMEM((q + 8, gw + 2 * ns), F32),
                        pltpu.VMEM((ns, gw), F32)],
        compiler_params=_cparams(("parallel", "parallel", "arbitrary")),
        name="ssd",
    )(u, u, u, u, dt_rows, conv_w, conv_w, conv_w, conv_b, conv_b, conv_b,
      dt_bias, a_log, d_skip, norm_w)


def _prep_kernel(q_ref, k_ref, v_ref, qi_ref, sm_ref, kpos_ref, qnw_ref, knw_ref, aug_ref,
                 qt_ref, ka_ref, vt_ref, qih_ref, kit_ref):
    tp = q_ref.shape[1]
    dh = ATT_HEAD_DIM
    qscale = (ATT_HEAD_DIM ** -0.5) * LOG2E

    def rms(xh, w):
        return xh * lax.rsqrt(jnp.mean(xh * xh, axis=-1, keepdims=True) + EPS) * w

    qf = q_ref[0].astype(F32)
    for h in range(ATT_HEADS):
        qh = rms(qf[:, h * dh:(h + 1) * dh], qnw_ref[...]) * qscale
        qt_ref[0, h, 0:dh, :] = jnp.transpose(qh).astype(BF16)
        qt_ref[0, h, dh:QK_AUG, :] = jnp.broadcast_to(aug_ref[h], (QK_AUG - dh, tp)).astype(BF16)

    kp = kpos_ref[0]
    lane = lax.broadcasted_iota(jnp.int32, (tp, QK_AUG - dh), 1)
    p_hi = (kp >> 7).astype(F32)
    p_lo = (kp & 127).astype(F32)
    pos_cols = jnp.where(lane < 3, p_hi, jnp.where(lane < 6, p_lo, 0.0)).astype(BF16)
    kf = k_ref[0].astype(F32)
    vf = v_ref[0].astype(F32)
    for g in range(ATT_KV_HEADS):
        ka_ref[0, g, :, 0:dh] = rms(kf[:, g * dh:(g + 1) * dh], knw_ref[...]).astype(BF16)
        ka_ref[0, g, :, dh:QK_AUG] = pos_cols
        vt_ref[0, g] = jnp.transpose(vf[:, g * dh:(g + 1) * dh]).astype(BF16)

    qif = qi_ref[0].astype(F32)
    for h in range(IDX_HEADS):
        qih_ref[0, h] = qif[:, h * IDX_HEAD_DIM:(h + 1) * IDX_HEAD_DIM].astype(BF16)
    sm_t = jnp.transpose(sm_ref[0])
    for j in range(tp // IDX_KC):
        kit_ref[0, j] = sm_t[SM_KIDX:SM_KIDX + IDX_HEAD_DIM, j * IDX_KC:(j + 1) * IDX_KC].astype(BF16)


def _alibi_aug():
    out = np.zeros((ATT_HEADS, QK_AUG - ATT_HEAD_DIM, 1), np.float32)
    for h in range(ATT_HEADS):
        s = np.float32(2.0 ** (-ALIBI_MAX_BIAS * (h + 1) / ATT_HEADS)) * np.float32(LOG2E)
        rem = np.float32(s)
        for i in range(3):
            piece = np.float32(np.asarray(rem, np.float32).astype(BF16))
            out[h, i, 0] = piece * np.float32(128.0)
            out[h, 3 + i, 0] = piece
            rem = np.float32(rem - piece)
    return jnp.asarray(out)


def _alibi_corr():
    s = [np.float32(2.0) * np.float32(2.0 ** (-ALIBI_MAX_BIAS * (h + 1) / ATT_HEADS)) * np.float32(LOG2E)
         for h in range(ATT_HEADS)]
    return jnp.asarray(np.array(s, np.float32).reshape(ATT_HEADS, 1, 1))


def _dsa_prep(u, small, kpos_col, q_norm, k_norm):
    bsz, L, _ = u.shape
    tp = min(PREP_T, L)
    nkc = L // IDX_KC
    return pl.pallas_call(
        _prep_kernel,
        out_shape=(jax.ShapeDtypeStruct((bsz, ATT_HEADS, QK_AUG, L), BF16),
                   jax.ShapeDtypeStruct((bsz, ATT_KV_HEADS, L, QK_AUG), BF16),
                   jax.ShapeDtypeStruct((bsz, ATT_KV_HEADS, ATT_HEAD_DIM, L), BF16),
                   jax.ShapeDtypeStruct((bsz, IDX_HEADS, L, IDX_HEAD_DIM), BF16),
                   jax.ShapeDtypeStruct((bsz, nkc, IDX_HEAD_DIM, IDX_KC), BF16)),
        grid=(bsz, L // tp),
        in_specs=[pl.BlockSpec((1, tp, ATT_WIDTH), lambda b, i: (b, i, OFF_Q // ATT_WIDTH)),
                  pl.BlockSpec((1, tp, ATT_KV_WIDTH), lambda b, i: (b, i, OFF_K // ATT_KV_WIDTH)),
                  pl.BlockSpec((1, tp, ATT_KV_WIDTH), lambda b, i: (b, i, OFF_V // ATT_KV_WIDTH)),
                  pl.BlockSpec((1, tp, IDX_HEADS * IDX_HEAD_DIM),
                               lambda b, i: (b, i, OFF_QI // (IDX_HEADS * IDX_HEAD_DIM))),
                  pl.BlockSpec((1, tp, LANES), lambda b, i: (b, i, 0)),
                  pl.BlockSpec((1, tp, 1), lambda b, i: (b, i, 0)),
                  pl.BlockSpec((1, ATT_HEAD_DIM), lambda b, i: (0, 0)),
                  pl.BlockSpec((1, ATT_HEAD_DIM), lambda b, i: (0, 0)),
                  pl.BlockSpec((ATT_HEADS, QK_AUG - ATT_HEAD_DIM, 1), lambda b, i: (0, 0, 0))],
        out_specs=(pl.BlockSpec((1, ATT_HEADS, QK_AUG, tp), lambda b, i: (b, 0, 0, i)),
                   pl.BlockSpec((1, ATT_KV_HEADS, tp, QK_AUG), lambda b, i: (b, 0, i, 0)),
                   pl.BlockSpec((1, ATT_KV_HEADS, ATT_HEAD_DIM, tp), lambda b, i: (b, 0, 0, i)),
                   pl.BlockSpec((1, IDX_HEADS, tp, IDX_HEAD_DIM), lambda b, i: (b, 0, i, 0)),
                   pl.BlockSpec((1, tp // IDX_KC, IDX_HEAD_DIM, IDX_KC), lambda b, i: (b, i, 0, 0))),
        compiler_params=_cparams(("parallel", "parallel")),
        name="dsa_prep",
    )(u, u, u, u, small, kpos_col, q_norm.reshape(1, -1), k_norm.reshape(1, -1), _alibi_aug())


_INT_MIN = -2 ** 31
_NEG_INF_KEY = -2139095041


def _indexer_kernel(qih_ref, kit_ref, sm_ref, qpos_ref, kposr_ref, o_ref,
                    keys_ref, res_ref, wb_ref, thr_ref, *, top_k):
    qi = pl.program_id(1)
    tq = qih_ref.shape[2]
    kc_w = kit_ref.shape[3]
    nkc_all = kit_ref.shape[1]
    hw = kc_w // 2
    n_kc = jnp.minimum(((qi + 1) * tq + kc_w - 1) // kc_w, nkc_all)
    idx_scale = (IDX_HEAD_DIM * IDX_HEADS) ** -0.5

    sm = sm_ref[0]
    for h in range(IDX_HEADS):
        wb_ref[h] = jnp.broadcast_to(sm[:, SM_WIDX + h:SM_WIDX + h + 1], (tq, LANES))
    qall = qih_ref[0].reshape(IDX_HEADS * tq, IDX_HEAD_DIM)

    def chunk_body(kc, carry):
        res_ref[...] = jnp.dot(qall, kit_ref[0, kc], preferred_element_type=F32)
        kchunk = kposr_ref[0, kc] >> 6

        def rb_body(rb, c2):
            r0 = pl.multiple_of(rb * IDX_RB, IDX_RB)
            acc0 = jnp.zeros((IDX_RB, hw), F32)
            acc1 = jnp.zeros((IDX_RB, hw), F32)
            for h in range(IDX_HEADS):
                rel = jnp.maximum(res_ref[pl.ds(h * tq + r0, IDX_RB), :], 0.0)
                w = wb_ref[h, pl.ds(r0, IDX_RB), :]
                acc0 = acc0 + w * rel[:, 0:hw]
                acc1 = acc1 + w * rel[:, hw:kc_w]
            score = jnp.concatenate([acc0, acc1], axis=1) * idx_scale
            qchunk = qpos_ref[0, pl.ds(r0, IDX_RB), :] >> 6
            score = jnp.where(kchunk <= qchunk, score, -jnp.inf)
            bits = pltpu.bitcast(score, jnp.int32)
            keys_ref[kc, pl.ds(r0, IDX_RB), :] = bits ^ ((bits >> 31) & 0x7FFFFFFF)
            return c2

        lax.fori_loop(0, tq // IDX_RB, rb_body, 0)
        return carry

    lax.fori_loop(0, n_kc, chunk_body, 0)

    for sb in range(tq // IDX_SB):
        rows = pl.ds(sb * IDX_SB, IDX_SB)

        def bit_body(i, thr):
            cand = thr + jnp.left_shift(jnp.int32(1), 31 - i)
            candb = jnp.broadcast_to(cand, (IDX_SB, kc_w))

            def cnt_body(kc, acc):
                return acc + jnp.where(keys_ref[kc, rows, :] >= candb, 1, 0)

            acc = lax.fori_loop(0, n_kc, cnt_body, jnp.zeros((IDX_SB, kc_w), jnp.int32))
            cnt = jnp.sum(acc, axis=1, keepdims=True)
            return jnp.where(cnt >= top_k, cand, thr)

        thr = lax.fori_loop(0, 32, bit_body, jnp.full((IDX_SB, 1), _INT_MIN, jnp.int32))
        thr = jnp.maximum(thr, _NEG_INF_KEY + 1)
        thr_ref[rows, :] = jnp.broadcast_to(thr, (IDX_SB, LANES))

    thr_b = thr_ref[...]
    thr_full = jnp.concatenate([thr_b] * (kc_w // LANES), axis=1)

    def mask_body(kc, carry):
        m = jnp.where(keys_ref[kc] >= thr_full, 0.0, NEG_BIG).astype(F32)
        r0 = pl.multiple_of(kc * kc_w, kc_w)
        o_ref[0, pl.ds(r0, kc_w), :] = jnp.transpose(m).astype(BF16)
        return carry

    lax.fori_loop(0, n_kc, mask_body, 0)

    def fill_body(kc, carry):
        r0 = pl.multiple_of(kc * kc_w, kc_w)
        o_ref[0, pl.ds(r0, kc_w), :] = jnp.full((kc_w, tq), NEG_BIG, BF16)
        return carry

    lax.fori_loop(n_kc, nkc_all, fill_body, 0)


def _indexer(qih, kit, small, qpos_col, kpos_rows, top_k):
    bsz, _, L, _ = qih.shape
    tq = min(IDX_TQ, L)
    nkc = L // IDX_KC
    return pl.pallas_call(
        functools.partial(_indexer_kernel, top_k=top_k),
        out_shape=jax.ShapeDtypeStruct((bsz, L, L), BF16),
        grid=(bsz, L // tq),
        in_specs=[pl.BlockSpec((1, IDX_HEADS, tq, IDX_HEAD_DIM), lambda b, i: (b, 0, i, 0)),
                  pl.BlockSpec((1, nkc, IDX_HEAD_DIM, IDX_KC), lambda b, i: (b, 0, 0, 0)),
                  pl.BlockSpec((1, tq, LANES), lambda b, i: (b, i, 0)),
                  pl.BlockSpec((1, tq, 1), lambda b, i: (b, i, 0)),
                  pl.BlockSpec((1, nkc, 1, IDX_KC), lambda b, i: (b, 0, 0, 0))],
        out_specs=pl.BlockSpec((1, L, tq), lambda b, i: (b, 0, i)),
        scratch_shapes=[pltpu.VMEM((nkc, tq, IDX_KC), jnp.int32),
                        pltpu.VMEM((IDX_HEADS * tq, IDX_KC), F32),
                        pltpu.VMEM((IDX_HEADS, tq, LANES), F32),
                        pltpu.VMEM((tq, LANES), jnp.int32)],
        compiler_params=_cparams(("parallel", "arbitrary")),
        name="indexer",
    )(qih, kit, small, qpos_col, kpos_rows)


def _attn_kernel(qt_ref, ka_ref, vt_ref, mk_ref, kpos_ref, qpos_ref, corr_ref, o_ref,
                 m_ref, l_ref, acc_ref):
    qi = pl.program_id(1)
    ki = pl.program_id(2)
    tq = qt_ref.shape[3]
    tk = ka_ref.shape[2]
    kmax = ((qi + 1) * tq - 1) // tk

    @pl.when(ki == 0)
    def _():
        m_ref[...] = jnp.full(m_ref.shape, -jnp.inf, F32)
        l_ref[...] = jnp.zeros(l_ref.shape, F32)
        acc_ref[...] = jnp.zeros(acc_ref.shape, F32)

    def step(diag):
        bias = mk_ref[0].astype(F32)
        if diag:
            later = jnp.maximum(kpos_ref[0] - qpos_ref[0], 0).astype(F32)

        def head_body(h, carry):
            g = h // ATT_RQ
            s = jnp.dot(ka_ref[0, g], qt_ref[0, h], preferred_element_type=F32) + bias
            if diag:
                s = s - corr_ref[h] * later
            m_old = m_ref[h]
            m_new = jnp.maximum(m_old, jnp.max(s, axis=0, keepdims=True))
            alpha = jnp.exp2(m_old - m_new)
            p = jnp.exp2(s - m_new)
            l_ref[h] = alpha * l_ref[h] + jnp.sum(p, axis=0, keepdims=True)
            acc_ref[h] = acc_ref[h] * alpha + jnp.dot(vt_ref[0, g], p.astype(BF16),
                                                      preferred_element_type=F32)
            m_ref[h] = m_new
            return carry

        lax.fori_loop(0, ATT_HEADS, head_body, 0)

    is_diag = (ki + 1) * tk - 1 > qi * tq
    live = ki <= kmax

    @pl.when(jnp.logical_and(live, is_diag))
    def _():
        step(True)

    @pl.when(jnp.logical_and(live, jnp.logical_not(is_diag)))
    def _():
        step(False)

    @pl.when(ki == kmax)
    def _():
        for h in range(ATT_HEADS):
            out = acc_ref[h] / l_ref[h]
            o_ref[0, :, h * ATT_HEAD_DIM:(h + 1) * ATT_HEAD_DIM] = jnp.transpose(out).astype(BF16)


def _attn(qt, ka, vt, mask_t, kpos_col, qpos_row):
    bsz, _, _, L = qt.shape
    tq, tk = min(ATT_TQ, L), min(ATT_TK, L)

    def kclamp(i, k):
        return jnp.minimum(k, ((i + 1) * tq - 1) // tk)

    return pl.pallas_call(
        _attn_kernel,
        out_shape=jax.ShapeDtypeStruct((bsz, L, ATT_WIDTH), BF16),
        grid=(bsz, L // tq, L // tk),
        in_specs=[pl.BlockSpec((1, ATT_HEADS, QK_AUG, tq), lambda b, i, k: (b, 0, 0, i)),
                  pl.BlockSpec((1, ATT_KV_HEADS, tk, QK_AUG), lambda b, i, k: (b, 0, kclamp(i, k), 0)),
                  pl.BlockSpec((1, ATT_KV_HEADS, ATT_HEAD_DIM, tk), lambda b, i, k: (b, 0, 0, kclamp(i, k))),
                  pl.BlockSpec((1, tk, tq), lambda b, i, k: (b, kclamp(i, k), i)),
                  pl.BlockSpec((1, tk, 1), lambda b, i, k: (b, kclamp(i, k), 0)),
                  pl.BlockSpec((1, 1, tq), lambda b, i, k: (b, 0, i)),
                  pl.BlockSpec((ATT_HEADS, 1, 1), lambda b, i, k: (0, 0, 0))],
        out_specs=pl.BlockSpec((1, tq, ATT_WIDTH), lambda b, i, k: (b, i, 0)),
        scratch_shapes=[pltpu.VMEM((ATT_HEADS, 1, tq), F32),
                        pltpu.VMEM((ATT_HEADS, 1, tq), F32),
                        pltpu.VMEM((ATT_HEADS, ATT_HEAD_DIM, tq), F32)],
        compiler_params=_cparams(("parallel", "parallel", "arbitrary")),
        name="attn",
    )(qt, ka, vt, mask_t, kpos_col, qpos_row, _alibi_corr())


def _merge_kernel(ys_ref, ya_ref, g0_ref, g1_ref, ws_ref, wa_ref, o_ref):
    ps = jnp.dot(ys_ref[0], ws_ref[...], preferred_element_type=F32)
    pa = jnp.dot(ya_ref[0], wa_ref[...], preferred_element_type=F32)
    g0 = _sigmoid(g0_ref[0].astype(F32))
    g1 = _sigmoid(g1_ref[0].astype(F32))
    o_ref[0] = (g0 * ps + g1 * pa).astype(BF16)


def _merge(y_ssd, y_att, u, w_s, w_a):
    bsz, L, _ = y_ssd.shape
    d = w_s.shape[1]
    tm, tn = min(MRG_TM, L), min(MRG_TN, d)
    g0b = OFF_GATE // tn
    g1b = (OFF_GATE + d) // tn
    return pl.pallas_call(
        _merge_kernel,
        out_shape=jax.ShapeDtypeStruct((bsz, L, d), BF16),
        grid=(bsz, L // tm, d // tn),
        in_specs=[pl.BlockSpec((1, tm, SSD_D_INNER), lambda b, i, j: (b, i, 0)),
                  pl.BlockSpec((1, tm, ATT_WIDTH), lambda b, i, j: (b, i, 0)),
                  pl.BlockSpec((1, tm, tn), lambda b, i, j: (b, i, g0b + j)),
                  pl.BlockSpec((1, tm, tn), lambda b, i, j: (b, i, g1b + j)),
                  pl.BlockSpec((SSD_D_INNER, tn), lambda b, i, j: (0, j)),
                  pl.BlockSpec((ATT_WIDTH, tn), lambda b, i, j: (0, j))],
        out_specs=pl.BlockSpec((1, tm, tn), lambda b, i, j: (b, i, j)),
        compiler_params=_cparams(("parallel", "parallel", "arbitrary")),
        name="merge",
    )(y_ssd, y_att, u, u, w_s, w_a)


def _outproj_kernel(m_ref, h_ref, g_ref, w_ref, o_ref):
    o_ref[0] = h_ref[0] + g_ref[0] * jnp.dot(m_ref[0], w_ref[...], preferred_element_type=F32)


def _outproj(merged, h, gate, w_out):
    bsz, L, d = h.shape
    tm, tn = min(MRG_TM, L), min(MRG_TN, d)
    return pl.pallas_call(
        _outproj_kernel,
        out_shape=jax.ShapeDtypeStruct((bsz, L, d), F32),
        grid=(bsz, L // tm, d // tn),
        in_specs=[pl.BlockSpec((1, tm, d), lambda b, i, j: (b, i, 0)),
                  pl.BlockSpec((1, tm, tn), lambda b, i, j: (b, i, j)),
                  pl.BlockSpec((1, 1, tn), lambda b, i, j: (b, 0, j)),
                  pl.BlockSpec((d, tn), lambda b, i, j: (0, j))],
        out_specs=pl.BlockSpec((1, tm, tn), lambda b, i, j: (b, i, j)),
        compiler_params=_cparams(("parallel", "parallel", "arbitrary")),
        name="outproj",
    )(merged, h, gate, w_out)


def _split_w_in(w_in, d):
    sizes = (SSD_D_INNER, SSD_D_INNER + 2 * SSD_GROUPS * SSD_STATE, SSD_HEADS, ATT_WIDTH, ATT_KV_WIDTH,
             ATT_KV_WIDTH, IDX_HEADS * IDX_HEAD_DIM, IDX_HEAD_DIM, IDX_HEADS, 2 * d)
    offs = np.concatenate([[0], np.cumsum(sizes)])
    seg = [w_in[:, int(offs[i]):int(offs[i + 1])] for i in range(len(sizes))]
    w_z, w_xbc, w_dt, w_q, w_k, w_v, w_qi, w_ki, w_wi, w_g = seg
    w_main = jnp.concatenate([w_z, w_xbc, w_q, w_k, w_v, w_qi, w_g], axis=1).astype(BF16)
    pad = jnp.zeros((w_in.shape[0], LANES - (IDX_HEAD_DIM + IDX_HEADS + SSD_HEADS)), w_in.dtype)
    w_small = jnp.concatenate([w_ki, w_wi, w_dt, pad], axis=1).astype(BF16)
    return w_main, w_small


def kernel(x, c, positions, w_ada, b_ada, norm_ffn1, ffn1_w1, ffn1_w3, ffn1_w2, norm_mix, w_in,
           ssd_conv_w, ssd_conv_b, ssd_dt_bias, ssd_a_log, ssd_d, ssd_norm, q_norm, k_norm,
           w_br_ssd, w_br_att, w_out, norm_ffn2, ffn2_w1, ffn2_w3, ffn2_w2):
    bsz, L, d = x.shape
    depth = w_ada.shape[0]
    top_k = min(TOPK_MAX, L // 4)
    G, R = SSD_GROUPS, SSD_R
    kpos_col = positions.reshape(bsz, L, 1)
    qpos_row = positions.reshape(bsz, 1, L)
    kpos_rows = positions.reshape(bsz, L // IDX_KC, 1, IDX_KC)

    def pad_rows(a):
        return jnp.pad(a.reshape(G, R), ((0, 0), (0, 8 - R))).reshape(G, 8, 1)

    h = x
    for l in range(depth):
        mod = _ada(c, w_ada[l], b_ada[l]).reshape(bsz, N_MOD, 1, d)
        sh1, sc1, g1, sh2, sc2, g2, sh3, sc3, g3 = [mod[:, i] for i in range(N_MOD)]

        h = _ffn(h, norm_ffn1[l].reshape(1, d), sh1, sc1, g1,
                 ffn1_w1[l].astype(BF16), ffn1_w3[l].astype(BF16), ffn1_w2[l].astype(BF16))

        w_main, w_small = _split_w_in(w_in[l], d)
        u, small = _inproj(h, norm_mix[l].reshape(1, d), sh2, sc2, w_main, w_small)

        dt_raw = small[:, :, SM_DT:SM_DT + SSD_HEADS].reshape(bsz, L, G, R)
        dt_rows = jnp.pad(jnp.transpose(dt_raw, (0, 2, 3, 1)), ((0, 0), (0, 0), (0, 8 - R), (0, 0)))
        y_ssd = _ssd(u, dt_rows, ssd_conv_w[l], ssd_conv_b[l].reshape(1, -1),
                     pad_rows(ssd_dt_bias[l]), pad_rows(ssd_a_log[l]),
                     jnp.repeat(ssd_d[l], SSD_HEAD_DIM).reshape(1, SSD_D_INNER),
                     ssd_norm[l].reshape(1, SSD_D_INNER))

        qt, ka, vt, qih, kit = _dsa_prep(u, small, kpos_col, q_norm[l], k_norm[l])
        mask_t = _indexer(qih, kit, small, kpos_col, kpos_rows, top_k)
        y_att = _attn(qt, ka, vt, mask_t, kpos_col, qpos_row)

        merged = _merge(y_ssd, y_att, u, w_br_ssd[l].astype(BF16), w_br_att[l].astype(BF16))
        h = _outproj(merged, h, g2, w_out[l].astype(BF16))

        h = _ffn(h, norm_ffn2[l].reshape(1, d), sh3, sc3, g3,
                 ffn2_w1[l].astype(BF16), ffn2_w3[l].astype(BF16), ffn2_w2[l].astype(BF16))
    return h
```

```python
import functools
import math

import numpy as np
import jax
import jax.numpy as jnp
from jax import lax
from jax.experimental import pallas as pl
from jax.experimental.pallas import tpu as pltpu

F32 = jnp.float32
BF16 = jnp.bfloat16

CHUNK = 64
EPS = 1e-6
N_MOD = 9
SSD_D_INNER = 2048
SSD_HEAD_DIM = 64
SSD_GROUPS = 8
SSD_HEADS = SSD_D_INNER // SSD_HEAD_DIM
SSD_R = SSD_HEADS // SSD_GROUPS
SSD_GW = SSD_D_INNER // SSD_GROUPS
SSD_STATE = 128
SSD_CONV = 4
ATT_HEADS = 16
ATT_KV_HEADS = 4
ATT_RQ = ATT_HEADS // ATT_KV_HEADS
ATT_HEAD_DIM = 128
ATT_WIDTH = ATT_HEADS * ATT_HEAD_DIM
ATT_KV_WIDTH = ATT_KV_HEADS * ATT_HEAD_DIM
IDX_HEADS = 16
IDX_HEAD_DIM = 64
TOPK_MAX = 256
ALIBI_MAX_BIAS = 8.0
LOG2E = 1.4426950408889634
NEG_BIG = -1e30

LANES = 128
QK_AUG = 256
VT_ROWS = ATT_HEAD_DIM + 16
VMEM_LIMIT = 56 * 1024 * 1024

OFF_Z = 0
OFF_XBC = OFF_Z + SSD_D_INNER
OFF_Q = OFF_XBC + SSD_D_INNER + 2 * SSD_GROUPS * SSD_STATE
OFF_K = OFF_Q + ATT_WIDTH
OFF_V = OFF_K + ATT_KV_WIDTH
OFF_QI = OFF_V + ATT_KV_WIDTH
OFF_GATE = OFF_QI + IDX_HEADS * IDX_HEAD_DIM
SM_KIDX = 0
SM_WIDX = IDX_HEAD_DIM
SM_DT = SM_WIDX + IDX_HEADS

FFN_TM, FFN_TF = 512, 512
NORM_RB = 128
INP_TM, INP_TN = 1024, 1024
SSD_Q = 256
PREP_T = 256
IDX_TQ, IDX_KC, IDX_RB = 128, 256, 16
ATT_TQ, ATT_TK = 256, 512
ATT_LOOKAHEAD = 2
MRG_TM, MRG_TN = 512, 512


def _cparams(sem):
    return pltpu.CompilerParams(dimension_semantics=sem, vmem_limit_bytes=VMEM_LIMIT)


def _sigmoid(x):
    return 1.0 / (1.0 + jnp.exp(-x))


def _rms_mod(x, nw, sh, sc):
    ms = jnp.mean(x * x, axis=-1, keepdims=True)
    return (x * lax.rsqrt(ms + EPS) * nw) * (1.0 + sc) + sh


def _rms_mod_rows(x_ref, nw_ref, sh_ref, sc_ref, hn_ref):
    tm = hn_ref.shape[0]
    rb = min(NORM_RB, tm)

    def body(i, carry):
        rows = pl.ds(pl.multiple_of(i * rb, rb), rb)
        hn_ref[rows, :] = _rms_mod(x_ref[0, rows, :], nw_ref[...], sh_ref[0], sc_ref[0]).astype(BF16)
        return carry

    lax.fori_loop(0, tm // rb, body, 0)


def _ada_kernel(ct_ref, w_ref, b_ref, o_ref):
    ct = ct_ref[...]
    ca = ct * _sigmoid(ct)
    w = w_ref[...]
    rows = [jnp.sum(w * ca[:, b:b + 1], axis=0, keepdims=True) for b in range(ct.shape[1])]
    o_ref[...] = jnp.concatenate(rows, axis=0) + b_ref[...]


def _ada(c, w_ada, b_ada):
    bsz, d = c.shape
    n = w_ada.shape[1]
    tn = 1024 if n % 1024 == 0 else n
    return pl.pallas_call(
        _ada_kernel,
        out_shape=jax.ShapeDtypeStruct((bsz, n), F32),
        grid=(n // tn,),
        in_specs=[pl.BlockSpec((d, bsz), lambda j: (0, 0)),
                  pl.BlockSpec((d, tn), lambda j: (0, j)),
                  pl.BlockSpec((1, tn), lambda j: (0, j))],
        out_specs=pl.BlockSpec((bsz, tn), lambda j: (0, j)),
        compiler_params=_cparams(("arbitrary",)),
        name="ada",
    )(c.T, w_ada, b_ada.reshape(1, n))


def _ffn_kernel(x_ref, nw_ref, sh_ref, sc_ref, g_ref, w1_ref, w3_ref, w2_ref, o_ref, hn_ref):
    f = pl.program_id(2)

    @pl.when(f == 0)
    def _():
        _rms_mod_rows(x_ref, nw_ref, sh_ref, sc_ref, hn_ref)
        o_ref[0] = jnp.zeros(o_ref.shape[1:], F32)

    hn = hn_ref[...]
    a = jnp.dot(hn, w1_ref[...], preferred_element_type=F32)
    b = jnp.dot(hn, w3_ref[...], preferred_element_type=F32)
    g = (a * _sigmoid(a) * b).astype(BF16)
    o_ref[0] += jnp.dot(g, w2_ref[...], preferred_element_type=F32)

    @pl.when(f == pl.num_programs(2) - 1)
    def _():
        o_ref[0] = x_ref[0] + 0.5 * g_ref[0] * o_ref[0]


def _ffn(h, nw, sh, sc, gate, w1, w3, w2):
    bsz, L, d = h.shape
    ff = w1.shape[1]
    tm, tf = min(FFN_TM, L), min(FFN_TF, ff)
    vec = pl.BlockSpec((1, 1, d), lambda b, i, f: (b, 0, 0))
    return pl.pallas_call(
        _ffn_kernel,
        out_shape=jax.ShapeDtypeStruct((bsz, L, d), F32),
        grid=(bsz, L // tm, ff // tf),
        in_specs=[pl.BlockSpec((1, tm, d), lambda b, i, f: (b, i, 0)),
                  pl.BlockSpec((1, d), lambda b, i, f: (0, 0)),
                  vec, vec, vec,
                  pl.BlockSpec((d, tf), lambda b, i, f: (0, f)),
                  pl.BlockSpec((d, tf), lambda b, i, f: (0, f)),
                  pl.BlockSpec((tf, d), lambda b, i, f: (f, 0))],
        out_specs=pl.BlockSpec((1, tm, d), lambda b, i, f: (b, i, 0)),
        scratch_shapes=[pltpu.VMEM((tm, d), BF16)],
        compiler_params=_cparams(("parallel", "parallel", "arbitrary")),
        name="ffn",
    )(h, nw, sh, sc, gate, w1, w3, w2)


def _inproj_kernel(x_ref, nw_ref, sh_ref, sc_ref, w_ref, ws_ref, u_ref, s_ref, hn_ref):
    j = pl.program_id(2)

    @pl.when(j == 0)
    def _():
        _rms_mod_rows(x_ref, nw_ref, sh_ref, sc_ref, hn_ref)
        s_ref[0] = jnp.dot(hn_ref[...], ws_ref[...], preferred_element_type=F32)

    u_ref[0] = jnp.dot(hn_ref[...], w_ref[...], preferred_element_type=F32).astype(BF16)


def _inproj(h, nw, sh, sc, w_main, w_small):
    bsz, L, d = h.shape
    n = w_main.shape[1]
    tm = min(INP_TM, L)
    tn = INP_TN if n % INP_TN == 0 else INP_TN // 2
    vec = pl.BlockSpec((1, 1, d), lambda b, i, j: (b, 0, 0))
    return pl.pallas_call(
        _inproj_kernel,
        out_shape=(jax.ShapeDtypeStruct((bsz, L, n), BF16),
                   jax.ShapeDtypeStruct((bsz, L, LANES), F32)),
        grid=(bsz, L // tm, n // tn),
        in_specs=[pl.BlockSpec((1, tm, d), lambda b, i, j: (b, i, 0)),
                  pl.BlockSpec((1, d), lambda b, i, j: (0, 0)),
                  vec, vec,
                  pl.BlockSpec((d, tn), lambda b, i, j: (0, j)),
                  pl.BlockSpec((d, LANES), lambda b, i, j: (0, 0))],
        out_specs=(pl.BlockSpec((1, tm, tn), lambda b, i, j: (b, i, j)),
                   pl.BlockSpec((1, tm, LANES), lambda b, i, j: (b, i, 0))),
        scratch_shapes=[pltpu.VMEM((tm, d), BF16)],
        compiler_params=_cparams(("parallel", "parallel", "arbitrary")),
        name="inproj",
    )(h, nw, sh, sc, w_main, w_small)


def _ssd_kernel(xs_ref, bm_ref, cm_ref, z_ref, dtr_ref, cwx_ref, cwb_ref, cwc_ref,
                cbx_ref, cbb_ref, cbc_ref, dtb_ref, alog_ref, dsk_ref, nw_ref,
                o_ref, ext_ref, state_ref):
    t = pl.program_id(2)
    q = xs_ref.shape[1]
    gw, ns = SSD_GW, SSD_STATE
    cw = gw + 2 * ns

    @pl.when(t == 0)
    def _():
        ext_ref[0:8, :] = jnp.zeros((8, cw), F32)
        state_ref[...] = jnp.zeros(state_ref.shape, F32)

    ext_ref[8:8 + q, 0:gw] = xs_ref[0].astype(F32)
    ext_ref[8:8 + q, gw:gw + ns] = bm_ref[0].astype(F32)
    ext_ref[8:8 + q, gw + ns:cw] = cm_ref[0].astype(F32)
    wts = jnp.concatenate([cwx_ref[...], cwb_ref[...], cwc_ref[...]], axis=1)
    acc = jnp.concatenate([cbx_ref[...], cbb_ref[...], cbc_ref[...]], axis=1)
    for j in range(SSD_CONV):
        acc = acc + wts[j:j + 1, :] * ext_ref[8 - (SSD_CONV - 1) + j:8 - (SSD_CONV - 1) + j + q, :]
    tail = ext_ref[q:q + 8, :]
    ext_ref[0:8, :] = tail
    xc = acc * _sigmoid(acc)
    xs = xc[:, 0:gw]
    bm = xc[:, gw:gw + ns].astype(BF16)
    cm = xc[:, gw + ns:cw].astype(BF16)
    xs_b = xs.astype(BF16)

    dtx = dtr_ref[0, 0] + dtb_ref[0]
    dt_row = jnp.maximum(dtx, 0.0) + jnp.log(1.0 + jnp.exp(-jnp.abs(dtx)))
    dta_row = dt_row * (-jnp.exp(alog_ref[0]))

    ti = lax.broadcasted_iota(jnp.int32, (q, q), 0)
    si = lax.broadcasted_iota(jnp.int32, (q, q), 1)
    tril = si <= ti
    eye = si == ti
    lane_head = lax.broadcasted_iota(jnp.int32, (1, gw), 1) // SSD_HEAD_DIM

    cb = lax.dot_general(cm, bm, (((1,), (1,)), ((), ())), preferred_element_type=F32)

    y = jnp.zeros((q, gw), F32)
    f_exp = jnp.zeros((q, gw), F32)
    f_w = jnp.zeros((q, gw), F32)
    e_dec = jnp.zeros((1, gw), F32)
    for r in range(SSD_R):
        dta_r = dta_row[r:r + 1, :]
        dt_r = dt_row[r:r + 1, :]
        acum_c = jnp.sum(jnp.where(tril, dta_r, 0.0), axis=1, keepdims=True)
        acum_r = jnp.sum(jnp.where(eye, acum_c, 0.0), axis=0, keepdims=True)
        dt_c = jnp.sum(jnp.where(eye, dt_r, 0.0), axis=1, keepdims=True)
        decay = jnp.exp(jnp.where(tril, acum_c - acum_r, -jnp.inf))
        m_r = (cb * decay * dt_r).astype(BF16)
        yd = jnp.dot(m_r, xs_b, preferred_element_type=F32)
        sel = lane_head == r
        y = y + jnp.where(sel, yd, 0.0)
        a_last = acum_r[:, q - 1:q]
        f_exp = f_exp + jnp.where(sel, jnp.exp(acum_c), 0.0)
        f_w = f_w + jnp.where(sel, dt_c * jnp.exp(a_last - acum_c), 0.0)
        e_dec = e_dec + jnp.where(sel, jnp.exp(a_last), 0.0)

    state = state_ref[...]
    y = y + jnp.dot(cm, state.astype(BF16), preferred_element_type=F32) * f_exp
    xw = (xs * f_w).astype(BF16)
    bm_t = jnp.transpose(xc[:, gw:gw + ns]).astype(BF16)
    state_ref[...] = state * e_dec + jnp.dot(bm_t, xw, preferred_element_type=F32)

    y = y + dsk_ref[...] * xs
    zf = z_ref[0].astype(F32)
    y = y * (zf * _sigmoid(zf))
    ms = jnp.mean(y * y, axis=-1, keepdims=True)
    o_ref[0] = (y * lax.rsqrt(ms + EPS) * nw_ref[...]).astype(BF16)


def _ssd(u, dt_rows, conv_w, conv_b, dt_bias, a_log, d_skip, norm_w):
    bsz, L, _ = u.shape
    q = min(SSD_Q, L)
    gw, ns, G = SSD_GW, SSD_STATE, SSD_GROUPS
    xb = OFF_XBC // gw
    bb = (OFF_XBC + SSD_D_INNER) // ns
    cb = bb + G
    cwb0 = SSD_D_INNER // ns
    return pl.pallas_call(
        _ssd_kernel,
        out_shape=jax.ShapeDtypeStruct((bsz, L, SSD_D_INNER), BF16),
        grid=(bsz, G, L // q),
        in_specs=[pl.BlockSpec((1, q, gw), lambda b, g, t: (b, t, xb + g)),
                  pl.BlockSpec((1, q, ns), lambda b, g, t: (b, t, bb + g)),
                  pl.BlockSpec((1, q, ns), lambda b, g, t: (b, t, cb + g)),
                  pl.BlockSpec((1, q, gw), lambda b, g, t: (b, t, g)),
                  pl.BlockSpec((1, 1, 8, q), lambda b, g, t: (b, g, 0, t)),
                  pl.BlockSpec((SSD_CONV, gw), lambda b, g, t: (0, g)),
                  pl.BlockSpec((SSD_CONV, ns), lambda b, g, t: (0, cwb0 + g)),
                  pl.BlockSpec((SSD_CONV, ns), lambda b, g, t: (0, cwb0 + G + g)),
                  pl.BlockSpec((1, gw), lambda b, g, t: (0, g)),
                  pl.BlockSpec((1, ns), lambda b, g, t: (0, cwb0 + g)),
                  pl.BlockSpec((1, ns), lambda b, g, t: (0, cwb0 + G + g)),
                  pl.BlockSpec((1, 8, 1), lambda b, g, t: (g, 0, 0)),
                  pl.BlockSpec((1, 8, 1), lambda b, g, t: (g, 0, 0)),
                  pl.BlockSpec((1, gw), lambda b, g, t: (0, g)),
                  pl.BlockSpec((1, gw), lambda b, g, t: (0, g))],
        out_specs=pl.BlockSpec((1, q, gw), lambda b, g, t: (b, t, g)),
        scratch_shapes=[pltpu.VMEM((q + 8, gw + 2 * ns), F32),
                        pltpu.VMEM((ns, gw), F32)],
        compiler_params=_cparams(("parallel", "parallel", "arbitrary")),
        name="ssd",
    )(u, u, u, u, dt_rows, conv_w, conv_w, conv_w, conv_b, conv_b, conv_b,
      dt_bias, a_log, d_skip, norm_w)


def _prep_kernel(q_ref, k_ref, v_ref, qi_ref, sm_ref, kpos_ref, qnw_ref, knw_ref, aug_ref,
                 qt_ref, ka_ref, vt_ref, qih_ref, kit_ref):
    tp = q_ref.shape[1]
    dh = ATT_HEAD_DIM
    qscale = (ATT_HEAD_DIM ** -0.5) * LOG2E

    def rms(xh, w):
        return xh * lax.rsqrt(jnp.mean(xh * xh, axis=-1, keepdims=True) + EPS) * w

    qf = q_ref[0].astype(F32)
    for h in range(ATT_HEADS):
        qh = rms(qf[:, h * dh:(h + 1) * dh], qnw_ref[...]) * qscale
        qt_ref[0, h, 0:dh, :] = jnp.transpose(qh).astype(BF16)
        qt_ref[0, h, dh:QK_AUG, :] = jnp.broadcast_to(aug_ref[h], (QK_AUG - dh, tp)).astype(BF16)

    kp = kpos_ref[0]
    lane = lax.broadcasted_iota(jnp.int32, (tp, QK_AUG - dh), 1)
    p_hi = (kp >> 7).astype(F32)
    p_lo = (kp & 127).astype(F32)
    pos_cols = jnp.where(lane < 3, p_hi, jnp.where(lane < 6, p_lo, 0.0)).astype(BF16)
    kf = k_ref[0].astype(F32)
    vf = v_ref[0].astype(F32)
    for g in range(ATT_KV_HEADS):
        ka_ref[0, g, :, 0:dh] = rms(kf[:, g * dh:(g + 1) * dh], knw_ref[...]).astype(BF16)
        ka_ref[0, g, :, dh:QK_AUG] = pos_cols
        vt_ref[0, g, 0:dh, :] = jnp.transpose(vf[:, g * dh:(g + 1) * dh]).astype(BF16)
        vt_ref[0, g, dh:VT_ROWS, :] = jnp.ones((VT_ROWS - dh, tp), BF16)

    qif = qi_ref[0].astype(F32)
    for h in range(IDX_HEADS):
        qh = qif[:, h * IDX_HEAD_DIM:(h + 1) * IDX_HEAD_DIM]
        qih_ref[0, :, h] = qh.reshape(tp // IDX_RB, IDX_RB, IDX_HEAD_DIM).astype(BF16)
    sm_t = jnp.transpose(sm_ref[0])
    for j in range(tp // IDX_KC):
        kit_ref[0, j] = sm_t[SM_KIDX:SM_KIDX + IDX_HEAD_DIM, j * IDX_KC:(j + 1) * IDX_KC].astype(BF16)


def _alibi_aug():
    out = np.zeros((ATT_HEADS, QK_AUG - ATT_HEAD_DIM, 1), np.float32)
    for h in range(ATT_HEADS):
        s = np.float32(2.0 ** (-ALIBI_MAX_BIAS * (h + 1) / ATT_HEADS)) * np.float32(LOG2E)
        rem = np.float32(s)
        for i in range(3):
            piece = np.float32(np.asarray(rem, np.float32).astype(BF16))
            out[h, i, 0] = piece * np.float32(128.0)
            out[h, 3 + i, 0] = piece
            rem = np.float32(rem - piece)
    return jnp.asarray(out)


def _alibi_corr():
    s = [np.float32(2.0) * np.float32(2.0 ** (-ALIBI_MAX_BIAS * (h + 1) / ATT_HEADS)) * np.float32(LOG2E)
         for h in range(ATT_HEADS)]
    return jnp.asarray(np.array(s, np.float32).reshape(ATT_HEADS, 1, 1))


def _dsa_prep(u, small, kpos_col, q_norm, k_norm):
    bsz, L, _ = u.shape
    tp = min(PREP_T, L)
    nkc = L // IDX_KC
    return pl.pallas_call(
        _prep_kernel,
        out_shape=(jax.ShapeDtypeStruct((bsz, ATT_HEADS, QK_AUG, L), BF16),
                   jax.ShapeDtypeStruct((bsz, ATT_KV_HEADS, L, QK_AUG), BF16),
                   jax.ShapeDtypeStruct((bsz, ATT_KV_HEADS, VT_ROWS, L), BF16),
                   jax.ShapeDtypeStruct((bsz, L // IDX_RB, IDX_HEADS, IDX_RB, IDX_HEAD_DIM), BF16),
                   jax.ShapeDtypeStruct((bsz, nkc, IDX_HEAD_DIM, IDX_KC), BF16)),
        grid=(bsz, L // tp),
        in_specs=[pl.BlockSpec((1, tp, ATT_WIDTH), lambda b, i: (b, i, OFF_Q // ATT_WIDTH)),
                  pl.BlockSpec((1, tp, ATT_KV_WIDTH), lambda b, i: (b, i, OFF_K // ATT_KV_WIDTH)),
                  pl.BlockSpec((1, tp, ATT_KV_WIDTH), lambda b, i: (b, i, OFF_V // ATT_KV_WIDTH)),
                  pl.BlockSpec((1, tp, IDX_HEADS * IDX_HEAD_DIM),
                               lambda b, i: (b, i, OFF_QI // (IDX_HEADS * IDX_HEAD_DIM))),
                  pl.BlockSpec((1, tp, LANES), lambda b, i: (b, i, 0)),
                  pl.BlockSpec((1, tp, 1), lambda b, i: (b, i, 0)),
                  pl.BlockSpec((1, ATT_HEAD_DIM), lambda b, i: (0, 0)),
                  pl.BlockSpec((1, ATT_HEAD_DIM), lambda b, i: (0, 0)),
                  pl.BlockSpec((ATT_HEADS, QK_AUG - ATT_HEAD_DIM, 1), lambda b, i: (0, 0, 0))],
        out_specs=(pl.BlockSpec((1, ATT_HEADS, QK_AUG, tp), lambda b, i: (b, 0, 0, i)),
                   pl.BlockSpec((1, ATT_KV_HEADS, tp, QK_AUG), lambda b, i: (b, 0, i, 0)),
                   pl.BlockSpec((1, ATT_KV_HEADS, VT_ROWS, tp), lambda b, i: (b, 0, 0, i)),
                   pl.BlockSpec((1, tp // IDX_RB, IDX_HEADS, IDX_RB, IDX_HEAD_DIM),
                                lambda b, i: (b, i, 0, 0, 0)),
                   pl.BlockSpec((1, tp // IDX_KC, IDX_HEAD_DIM, IDX_KC), lambda b, i: (b, i, 0, 0))),
        compiler_params=_cparams(("parallel", "parallel")),
        name="dsa_prep",
    )(u, u, u, u, small, kpos_col, q_norm.reshape(1, -1), k_norm.reshape(1, -1), _alibi_aug())


_INT_MIN = -2 ** 31
_NEG_INF_KEY = -2139095041


def _tree_sum(terms):
    while len(terms) > 1:
        terms = [terms[i] + terms[i + 1] for i in range(0, len(terms) - 1, 2)] + (
            [terms[-1]] if len(terms) % 2 else [])
    return terms[0]


def _indexer_kernel(qih_ref, kit_ref, sm_ref, qpos_ref, kposr_ref, o_ref,
                    keys_ref, wb_ref, bmax_ref, *, top_k):
    qi = pl.program_id(1)
    nrb = qih_ref.shape[1]
    tq = nrb * IDX_RB
    kc_w = kit_ref.shape[3]
    nkc_all = kit_ref.shape[1]
    hw = kc_w // 2
    assert hw == LANES and top_k <= kc_w
    n_kc = jnp.minimum(((qi + 1) * tq + kc_w - 1) // kc_w, nkc_all)
    idx_scale = (IDX_HEAD_DIM * IDX_HEADS) ** -0.5

    sm = sm_ref[0]
    for h in range(IDX_HEADS):
        wb_ref[h] = jnp.broadcast_to(sm[:, SM_WIDX + h:SM_WIDX + h + 1], (tq, LANES))
    bmax_ref[...] = jnp.full(bmax_ref.shape, _INT_MIN, jnp.int32)

    def chunk_body(kc, carry):
        kt = kit_ref[0, kc]
        kchunk = kposr_ref[0, kc] >> 6

        def qk(rb):
            return jnp.dot(qih_ref[0, rb].reshape(IDX_HEADS * IDX_RB, IDX_HEAD_DIM), kt,
                           preferred_element_type=F32)

        s_next = qk(0)
        for rb in range(nrb):
            rows = slice(rb * IDX_RB, (rb + 1) * IDX_RB)
            s = s_next
            if rb + 1 < nrb:
                s_next = qk(rb + 1)
            t0, t1 = [], []
            for h in range(IDX_HEADS):
                rel = jnp.maximum(s[h * IDX_RB:(h + 1) * IDX_RB, :], 0.0)
                w = wb_ref[h, rows, :]
                t0.append(w * rel[:, 0:hw])
                t1.append(w * rel[:, hw:kc_w])
            score = jnp.concatenate([_tree_sum(t0), _tree_sum(t1)], axis=1) * idx_scale
            qchunk = qpos_ref[0, rows, :] >> 6
            score = jnp.where(kchunk <= qchunk, score, -jnp.inf)
            bits = pltpu.bitcast(score, jnp.int32)
            key = bits ^ ((bits >> 31) & 0x7FFFFFFF)
            keys_ref[kc, rows, :] = key
            bmax_ref[rows, :] = jnp.maximum(bmax_ref[rows, :], key)
        return carry

    lax.fori_loop(0, n_kc, chunk_body, 0)
    keys_ref[n_kc] = jnp.full((tq, kc_w), _INT_MIN, jnp.int32)

    bm = bmax_ref[...]
    lb = jnp.min(bm, axis=1, keepdims=True)
    ub = jnp.max(bm, axis=1, keepdims=True)
    span = ub - lb
    wrapped = span < 0
    lb = jnp.where(wrapped, _INT_MIN, lb)
    nb = jnp.where(wrapped, 32, 32 - lax.clz(span))
    nbits = jnp.max(nb)

    def bit_body(i, thr):
        cand = thr + jnp.left_shift(jnp.int32(1), nbits - 1 - i)
        candb = jnp.broadcast_to(cand, (tq, hw))

        def cnt_body(pair, acc):
            for kc in (2 * pair, 2 * pair + 1):
                k = keys_ref[kc]
                acc = acc + jnp.where(k[:, 0:hw] >= candb, 1, 0) + jnp.where(k[:, hw:kc_w] >= candb, 1, 0)
            return acc

        acc = lax.fori_loop(0, (n_kc + 1) // 2, cnt_body, jnp.zeros((tq, hw), jnp.int32))
        cnt = jnp.sum(acc, axis=1, keepdims=True)
        return jnp.where(jnp.logical_and(cnt >= top_k, cand > thr), cand, thr)

    thr = lax.fori_loop(0, nbits, bit_body, lb)
    thr = jnp.maximum(thr, _NEG_INF_KEY + 1)
    thr_b = jnp.broadcast_to(thr, (tq, hw))
    thr_full = jnp.concatenate([thr_b] * (kc_w // hw), axis=1)

    def mask_body(kc, carry):
        m = jnp.where(keys_ref[kc] >= thr_full, 0.0, NEG_BIG).astype(F32)
        r0 = pl.multiple_of(kc * kc_w, kc_w)
        o_ref[0, pl.ds(r0, kc_w), :] = jnp.transpose(m).astype(BF16)
        return carry

    lax.fori_loop(0, n_kc, mask_body, 0)

    def fill_body(kc, carry):
        r0 = pl.multiple_of(kc * kc_w, kc_w)
        o_ref[0, pl.ds(r0, kc_w), :] = jnp.full((kc_w, tq), NEG_BIG, BF16)
        return carry

    lax.fori_loop(n_kc, nkc_all, fill_body, 0)


def _indexer(qih, kit, small, qpos_col, kpos_rows, top_k):
    bsz, L = small.shape[0], small.shape[1]
    tq = min(IDX_TQ, L)
    nkc = L // IDX_KC
    return pl.pallas_call(
        functools.partial(_indexer_kernel, top_k=top_k),
        out_shape=jax.ShapeDtypeStruct((bsz, L, L), BF16),
        grid=(bsz, L // tq),
        in_specs=[pl.BlockSpec((1, tq // IDX_RB, IDX_HEADS, IDX_RB, IDX_HEAD_DIM),
                               lambda b, i: (b, i, 0, 0, 0)),
                  pl.BlockSpec((1, nkc, IDX_HEAD_DIM, IDX_KC), lambda b, i: (b, 0, 0, 0)),
                  pl.BlockSpec((1, tq, LANES), lambda b, i: (b, i, 0)),
                  pl.BlockSpec((1, tq, 1), lambda b, i: (b, i, 0)),
                  pl.BlockSpec((1, nkc, 1, IDX_KC), lambda b, i: (b, 0, 0, 0))],
        out_specs=pl.BlockSpec((1, L, tq), lambda b, i: (b, 0, i)),
        scratch_shapes=[pltpu.VMEM((nkc + 1, tq, IDX_KC), jnp.int32),
                        pltpu.VMEM((IDX_HEADS, tq, LANES), F32),
                        pltpu.VMEM((tq, IDX_KC), jnp.int32)],
        compiler_params=_cparams(("parallel", "arbitrary")),
        name="indexer",
    )(qih, kit, small, qpos_col, kpos_rows)


def _attn_kernel(qtab_ref, ktab_ref, qt_ref, ka_ref, vt_ref, mk_ref, kpos_ref, qpos_ref, corr_ref,
                 o_ref, bias_ref, s_ref, *state_refs):
    s_id = pl.program_id(1)
    qi = qtab_ref[s_id]
    ki = ktab_ref[s_id]
    tq = qt_ref.shape[3]
    tk = ka_ref.shape[2]
    dh = ATT_HEAD_DIM
    kmax = ((qi + 1) * tq - 1) // tk
    m_refs, acc_refs = state_refs[:ATT_RQ], state_refs[ATT_RQ:]

    @pl.when(ki == 0)
    def _():
        for r in range(ATT_RQ):
            m_refs[r][...] = jnp.full(m_refs[r].shape, -jnp.inf, F32)
            acc_refs[r][...] = jnp.zeros(acc_refs[r].shape, F32)

    def step(diag):
        bias_ref[...] = mk_ref[0].astype(F32)
        if diag:
            later = jnp.maximum(kpos_ref[0] - qpos_ref[0], 0).astype(F32)

        def scores(h):
            s = jnp.dot(ka_ref[0, h // ATT_RQ], qt_ref[0, h], preferred_element_type=F32) + bias_ref[...]
            if diag:
                s = s - corr_ref[h] * later
            return s

        for h0 in range(ATT_LOOKAHEAD):
            s_ref[h0] = scores(h0)

        def group_body(g, carry, last=False):
            vg = vt_ref[0, g]
            for r in range(ATT_RQ):
                h = g * ATT_RQ + r
                if not (last and r + ATT_LOOKAHEAD >= ATT_RQ):
                    s_ref[(r + ATT_LOOKAHEAD) % ATT_RQ] = scores(h + ATT_LOOKAHEAD)
                s = s_ref[r]
                m_old = m_refs[r][g]
                m_new = jnp.maximum(m_old, jnp.max(s, axis=0, keepdims=True))
                p = jnp.exp2(s - m_new).astype(BF16)
                acc_refs[r][g] = acc_refs[r][g] * jnp.exp2(m_old - m_new) + jnp.dot(
                    vg, p, preferred_element_type=F32)
                m_refs[r][g] = m_new
            return carry

        lax.fori_loop(0, ATT_KV_HEADS - 1, group_body, 0)
        group_body(ATT_KV_HEADS - 1, 0, last=True)

    is_diag = (ki + 1) * tk - 1 > qi * tq

    @pl.when(is_diag)
    def _():
        step(True)

    @pl.when(jnp.logical_not(is_diag))
    def _():
        step(False)

    @pl.when(ki == kmax)
    def _():
        for h in range(ATT_HEADS):
            acc = acc_refs[h % ATT_RQ][h // ATT_RQ]
            out = acc[0:dh, :] / acc[dh:dh + 1, :]
            o_ref[0, :, h * dh:(h + 1) * dh] = jnp.transpose(out).astype(BF16)


def _attn(qt, ka, vt, mask_t, kpos_col, qpos_row):
    bsz, _, _, L = qt.shape
    tq, tk = min(ATT_TQ, L), min(ATT_TK, L)
    pairs = [(i, k) for i in range(L // tq) for k in range(((i + 1) * tq - 1) // tk + 1)]
    qtab = jnp.asarray(np.array([p[0] for p in pairs], np.int32))
    ktab = jnp.asarray(np.array([p[1] for p in pairs], np.int32))
    grid_spec = pltpu.PrefetchScalarGridSpec(
        num_scalar_prefetch=2,
        grid=(bsz, len(pairs)),
        in_specs=[pl.BlockSpec((1, ATT_HEADS, QK_AUG, tq), lambda b, s, qt_, kt_: (b, 0, 0, qt_[s])),
                  pl.BlockSpec((1, ATT_KV_HEADS, tk, QK_AUG), lambda b, s, qt_, kt_: (b, 0, kt_[s], 0)),
                  pl.BlockSpec((1, ATT_KV_HEADS, VT_ROWS, tk), lambda b, s, qt_, kt_: (b, 0, 0, kt_[s])),
                  pl.BlockSpec((1, tk, tq), lambda b, s, qt_, kt_: (b, kt_[s], qt_[s])),
                  pl.BlockSpec((1, tk, 1), lambda b, s, qt_, kt_: (b, kt_[s], 0)),
                  pl.BlockSpec((1, 1, tq), lambda b, s, qt_, kt_: (b, 0, qt_[s])),
                  pl.BlockSpec((ATT_HEADS, 1, 1), lambda b, s, qt_, kt_: (0, 0, 0))],
        out_specs=pl.BlockSpec((1, tq, ATT_WIDTH), lambda b, s, qt_, kt_: (b, qt_[s], 0)),
        scratch_shapes=([pltpu.VMEM((tk, tq), F32), pltpu.VMEM((ATT_RQ, tk, tq), F32)]
                        + [pltpu.VMEM((ATT_KV_HEADS, 1, tq), F32) for _ in range(ATT_RQ)]
                        + [pltpu.VMEM((ATT_KV_HEADS, VT_ROWS, tq), F32) for _ in range(ATT_RQ)]))
    return pl.pallas_call(
        _attn_kernel,
        out_shape=jax.ShapeDtypeStruct((bsz, L, ATT_WIDTH), BF16),
        grid_spec=grid_spec,
        compiler_params=_cparams(("parallel", "arbitrary")),
        name="attn",
    )(qtab, ktab, qt, ka, vt, mask_t, kpos_col, qpos_row, _alibi_corr())


def _merge_kernel(ys_ref, ya_ref, g0_ref, g1_ref, ws_ref, wa_ref, o_ref):
    ps = jnp.dot(ys_ref[0], ws_ref[...], preferred_element_type=F32)
    pa = jnp.dot(ya_ref[0], wa_ref[...], preferred_element_type=F32)
    g0 = _sigmoid(g0_ref[0].astype(F32))
    g1 = _sigmoid(g1_ref[0].astype(F32))
    o_ref[0] = (g0 * ps + g1 * pa).astype(BF16)


def _merge(y_ssd, y_att, u, w_s, w_a):
    bsz, L, _ = y_ssd.shape
    d = w_s.shape[1]
    tm, tn = min(MRG_TM, L), min(MRG_TN, d)
    g0b = OFF_GATE // tn
    g1b = (OFF_GATE + d) // tn
    return pl.pallas_call(
        _merge_kernel,
        out_shape=jax.ShapeDtypeStruct((bsz, L, d), BF16),
        grid=(bsz, L // tm, d // tn),
        in_specs=[pl.BlockSpec((1, tm, SSD_D_INNER), lambda b, i, j: (b, i, 0)),
                  pl.BlockSpec((1, tm, ATT_WIDTH), lambda b, i, j: (b, i, 0)),
                  pl.BlockSpec((1, tm, tn), lambda b, i, j: (b, i, g0b + j)),
                  pl.BlockSpec((1, tm, tn), lambda b, i, j: (b, i, g1b + j)),
                  pl.BlockSpec((SSD_D_INNER, tn), lambda b, i, j: (0, j)),
                  pl.BlockSpec((ATT_WIDTH, tn), lambda b, i, j: (0, j))],
        out_specs=pl.BlockSpec((1, tm, tn), lambda b, i, j: (b, i, j)),
        compiler_params=_cparams(("parallel", "parallel", "arbitrary")),
        name="merge",
    )(y_ssd, y_att, u, u, w_s, w_a)


def _outproj_kernel(m_ref, h_ref, g_ref, w_ref, o_ref):
    o_ref[0] = h_ref[0] + g_ref[0] * jnp.dot(m_ref[0], w_ref[...], preferred_element_type=F32)


def _outproj(merged, h, gate, w_out):
    bsz, L, d = h.shape
    tm, tn = min(MRG_TM, L), min(MRG_TN, d)
    return pl.pallas_call(
        _outproj_kernel,
        out_shape=jax.ShapeDtypeStruct((bsz, L, d), F32),
        grid=(bsz, L // tm, d // tn),
        in_specs=[pl.BlockSpec((1, tm, d), lambda b, i, j: (b, i, 0)),
                  pl.BlockSpec((1, tm, tn), lambda b, i, j: (b, i, j)),
                  pl.BlockSpec((1, 1, tn), lambda b, i, j: (b, 0, j)),
                  pl.BlockSpec((d, tn), lambda b, i, j: (0, j))],
        out_specs=pl.BlockSpec((1, tm, tn), lambda b, i, j: (b, i, j)),
        compiler_params=_cparams(("parallel", "parallel", "arbitrary")),
        name="outproj",
    )(merged, h, gate, w_out)


def _split_w_in(w_in, d):
    sizes = (SSD_D_INNER, SSD_D_INNER + 2 * SSD_GROUPS * SSD_STATE, SSD_HEADS, ATT_WIDTH, ATT_KV_WIDTH,
             ATT_KV_WIDTH, IDX_HEADS * IDX_HEAD_DIM, IDX_HEAD_DIM, IDX_HEADS, 2 * d)
    offs = np.concatenate([[0], np.cumsum(sizes)])
    seg = [w_in[:, int(offs[i]):int(offs[i + 1])] for i in range(len(sizes))]
    w_z, w_xbc, w_dt, w_q, w_k, w_v, w_qi, w_ki, w_wi, w_g = seg
    w_main = jnp.concatenate([w_z, w_xbc, w_q, w_k, w_v, w_qi, w_g], axis=1).astype(BF16)
    pad = jnp.zeros((w_in.shape[0], LANES - (IDX_HEAD_DIM + IDX_HEADS + SSD_HEADS)), w_in.dtype)
    w_small = jnp.concatenate([w_ki, w_wi, w_dt, pad], axis=1).astype(BF16)
    return w_main, w_small


def kernel(x, c, positions, w_ada, b_ada, norm_ffn1, ffn1_w1, ffn1_w3, ffn1_w2, norm_mix, w_in,
           ssd_conv_w, ssd_conv_b, ssd_dt_bias, ssd_a_log, ssd_d, ssd_norm, q_norm, k_norm,
           w_br_ssd, w_br_att, w_out, norm_ffn2, ffn2_w1, ffn2_w3, ffn2_w2):
    bsz, L, d = x.shape
    depth = w_ada.shape[0]
    top_k = min(TOPK_MAX, L // 4)
    G, R = SSD_GROUPS, SSD_R
    kpos_col = positions.reshape(bsz, L, 1)
    qpos_row = positions.reshape(bsz, 1, L)
    kpos_rows = positions.reshape(bsz, L // IDX_KC, 1, IDX_KC)

    def pad_rows(a):
        return jnp.pad(a.reshape(G, R), ((0, 0), (0, 8 - R))).reshape(G, 8, 1)

    h = x
    for l in range(depth):
        mod = _ada(c, w_ada[l], b_ada[l]).reshape(bsz, N_MOD, 1, d)
        sh1, sc1, g1, sh2, sc2, g2, sh3, sc3, g3 = [mod[:, i] for i in range(N_MOD)]

        h = _ffn(h, norm_ffn1[l].reshape(1, d), sh1, sc1, g1,
                 ffn1_w1[l].astype(BF16), ffn1_w3[l].astype(BF16), ffn1_w2[l].astype(BF16))

        w_main, w_small = _split_w_in(w_in[l], d)
        u, small = _inproj(h, norm_mix[l].reshape(1, d), sh2, sc2, w_main, w_small)

        dt_raw = small[:, :, SM_DT:SM_DT + SSD_HEADS].reshape(bsz, L, G, R)
        dt_rows = jnp.pad(jnp.transpose(dt_raw, (0, 2, 3, 1)), ((0, 0), (0, 0), (0, 8 - R), (0, 0)))
        y_ssd = _ssd(u, dt_rows, ssd_conv_w[l], ssd_conv_b[l].reshape(1, -1),
                     pad_rows(ssd_dt_bias[l]), pad_rows(ssd_a_log[l]),
                     jnp.repeat(ssd_d[l], SSD_HEAD_DIM).reshape(1, SSD_D_INNER),
                     ssd_norm[l].reshape(1, SSD_D_INNER))

        qt, ka, vt, qih, kit = _dsa_prep(u, small, kpos_col, q_norm[l], k_norm[l])
        mask_t = _indexer(qih, kit, small, kpos_col, kpos_rows, top_k)
        y_att = _attn(qt, ka, vt, mask_t, kpos_col, qpos_row)

        merged = _merge(y_ssd, y_att, u, w_br_ssd[l].astype(BF16), w_br_att[l].astype(BF16))
        h = _outproj(merged, h, g2, w_out[l].astype(BF16))

        h = _ffn(h, norm_ffn2[l].reshape(1, d), sh3, sc3, g3,
                 ffn2_w1[l].astype(BF16), ffn2_w3[l].astype(BF16), ffn2_w2[l].astype(BF16))
    return h
```

```python
import functools
import math

import numpy as np
import jax
import jax.numpy as jnp
from jax import lax
from jax.experimental import pallas as pl
from jax.experimental.pallas import tpu as pltpu

F32 = jnp.float32
BF16 = jnp.bfloat16

CHUNK = 64
EPS = 1e-6
N_MOD = 9
SSD_D_INNER = 2048
SSD_HEAD_DIM = 64
SSD_GROUPS = 8
SSD_HEADS = SSD_D_INNER // SSD_HEAD_DIM
SSD_R = SSD_HEADS // SSD_GROUPS
SSD_GW = SSD_D_INNER // SSD_GROUPS
SSD_STATE = 128
SSD_CONV = 4
ATT_HEADS = 16
ATT_KV_HEADS = 4
ATT_RQ = ATT_HEADS // ATT_KV_HEADS
ATT_HEAD_DIM = 128
ATT_WIDTH = ATT_HEADS * ATT_HEAD_DIM
ATT_KV_WIDTH = ATT_KV_HEADS * ATT_HEAD_DIM
IDX_HEADS = 16
IDX_HEAD_DIM = 64
TOPK_MAX = 256
ALIBI_MAX_BIAS = 8.0
LOG2E = 1.4426950408889634
NEG_BIG = -1e30

LANES = 128
QK_AUG = 256
VT_ROWS = ATT_HEAD_DIM + 16
VMEM_LIMIT = 56 * 1024 * 1024

OFF_Z = 0
OFF_XBC = OFF_Z + SSD_D_INNER
OFF_Q = OFF_XBC + SSD_D_INNER + 2 * SSD_GROUPS * SSD_STATE
OFF_K = OFF_Q + ATT_WIDTH
OFF_V = OFF_K + ATT_KV_WIDTH
OFF_QI = OFF_V + ATT_KV_WIDTH
OFF_GATE = OFF_QI + IDX_HEADS * IDX_HEAD_DIM
SM_KIDX = 0
SM_WIDX = IDX_HEAD_DIM
SM_DT = SM_WIDX + IDX_HEADS

FFN_TM, FFN_TF = 512, 512
NORM_RB = 128
INP_TM, INP_TN = 1024, 1024
SSD_Q = 256
PREP_T = 256
IDX_TQ, IDX_KC, IDX_RB = 128, 256, 16
ATT_TQ, ATT_TK = 256, 512
ATT_LOOKAHEAD = 2
ATT_SLOTS = ATT_LOOKAHEAD + 1
MRG_TM, MRG_TN = 1024, 512


def _cparams(sem):
    return pltpu.CompilerParams(dimension_semantics=sem, vmem_limit_bytes=VMEM_LIMIT)


def _sigmoid(x):
    return 1.0 / (1.0 + jnp.exp(-x))


def _rms_mod(x, nw, sh, sc):
    ms = jnp.mean(x * x, axis=-1, keepdims=True)
    return (x * lax.rsqrt(ms + EPS) * nw) * (1.0 + sc) + sh


def _rms_mod_rows(x_ref, nw_ref, sh_ref, sc_ref, hn_ref):
    tm = hn_ref.shape[0]
    rb = min(NORM_RB, tm)

    def body(i, carry):
        rows = pl.ds(pl.multiple_of(i * rb, rb), rb)
        hn_ref[rows, :] = _rms_mod(x_ref[0, rows, :], nw_ref[...], sh_ref[0], sc_ref[0]).astype(BF16)
        return carry

    lax.fori_loop(0, tm // rb, body, 0)


def _ada_kernel(ct_ref, w_ref, b_ref, o_ref):
    ct = ct_ref[...]
    ca = ct * _sigmoid(ct)
    w = w_ref[...]
    rows = [jnp.sum(w * ca[:, b:b + 1], axis=0, keepdims=True) for b in range(ct.shape[1])]
    o_ref[...] = jnp.concatenate(rows, axis=0) + b_ref[...]


def _ada(c, w_ada, b_ada):
    bsz, d = c.shape
    n = w_ada.shape[1]
    tn = 1024 if n % 1024 == 0 else n
    return pl.pallas_call(
        _ada_kernel,
        out_shape=jax.ShapeDtypeStruct((bsz, n), F32),
        grid=(n // tn,),
        in_specs=[pl.BlockSpec((d, bsz), lambda j: (0, 0)),
                  pl.BlockSpec((d, tn), lambda j: (0, j)),
                  pl.BlockSpec((1, tn), lambda j: (0, j))],
        out_specs=pl.BlockSpec((bsz, tn), lambda j: (0, j)),
        compiler_params=_cparams(("arbitrary",)),
        name="ada",
    )(c.T, w_ada, b_ada.reshape(1, n))


def _ffn_kernel(x_ref, nw_ref, sh_ref, sc_ref, g_ref, w1_ref, w3_ref, w2_ref, o_ref, hn_ref):
    f = pl.program_id(2)

    @pl.when(f == 0)
    def _():
        _rms_mod_rows(x_ref, nw_ref, sh_ref, sc_ref, hn_ref)
        o_ref[0] = jnp.zeros(o_ref.shape[1:], F32)

    hn = hn_ref[...]
    a = jnp.dot(hn, w1_ref[...], preferred_element_type=F32)
    b = jnp.dot(hn, w3_ref[...], preferred_element_type=F32)
    g = (a * _sigmoid(a) * b).astype(BF16)
    o_ref[0] += jnp.dot(g, w2_ref[...], preferred_element_type=F32)

    @pl.when(f == pl.num_programs(2) - 1)
    def _():
        o_ref[0] = x_ref[0] + 0.5 * g_ref[0] * o_ref[0]


def _ffn(h, nw, sh, sc, gate, w1, w3, w2):
    bsz, L, d = h.shape
    ff = w1.shape[1]
    tm, tf = min(FFN_TM, L), min(FFN_TF, ff)
    vec = pl.BlockSpec((1, 1, d), lambda b, i, f: (b, 0, 0))
    return pl.pallas_call(
        _ffn_kernel,
        out_shape=jax.ShapeDtypeStruct((bsz, L, d), F32),
        grid=(bsz, L // tm, ff // tf),
        in_specs=[pl.BlockSpec((1, tm, d), lambda b, i, f: (b, i, 0)),
                  pl.BlockSpec((1, d), lambda b, i, f: (0, 0)),
                  vec, vec, vec,
                  pl.BlockSpec((d, tf), lambda b, i, f: (0, f)),
                  pl.BlockSpec((d, tf), lambda b, i, f: (0, f)),
                  pl.BlockSpec((tf, d), lambda b, i, f: (f, 0))],
        out_specs=pl.BlockSpec((1, tm, d), lambda b, i, f: (b, i, 0)),
        scratch_shapes=[pltpu.VMEM((tm, d), BF16)],
        compiler_params=_cparams(("parallel", "parallel", "arbitrary")),
        name="ffn",
    )(h, nw, sh, sc, gate, w1, w3, w2)


def _inproj_kernel(x_ref, nw_ref, sh_ref, sc_ref, w_ref, ws_ref, u_ref, s_ref, hn_ref):
    j = pl.program_id(2)

    @pl.when(j == 0)
    def _():
        _rms_mod_rows(x_ref, nw_ref, sh_ref, sc_ref, hn_ref)
        s_ref[0] = jnp.dot(hn_ref[...], ws_ref[...], preferred_element_type=F32)

    u_ref[0] = jnp.dot(hn_ref[...], w_ref[...], preferred_element_type=F32).astype(BF16)


def _inproj(h, nw, sh, sc, w_main, w_small):
    bsz, L, d = h.shape
    n = w_main.shape[1]
    tm = min(INP_TM, L)
    tn = INP_TN if n % INP_TN == 0 else INP_TN // 2
    vec = pl.BlockSpec((1, 1, d), lambda b, i, j: (b, 0, 0))
    return pl.pallas_call(
        _inproj_kernel,
        out_shape=(jax.ShapeDtypeStruct((bsz, L, n), BF16),
                   jax.ShapeDtypeStruct((bsz, L, LANES), F32)),
        grid=(bsz, L // tm, n // tn),
        in_specs=[pl.BlockSpec((1, tm, d), lambda b, i, j: (b, i, 0)),
                  pl.BlockSpec((1, d), lambda b, i, j: (0, 0)),
                  vec, vec,
                  pl.BlockSpec((d, tn), lambda b, i, j: (0, j)),
                  pl.BlockSpec((d, LANES), lambda b, i, j: (0, 0))],
        out_specs=(pl.BlockSpec((1, tm, tn), lambda b, i, j: (b, i, j)),
                   pl.BlockSpec((1, tm, LANES), lambda b, i, j: (b, i, 0))),
        scratch_shapes=[pltpu.VMEM((tm, d), BF16)],
        compiler_params=_cparams(("parallel", "parallel", "arbitrary")),
        name="inproj",
    )(h, nw, sh, sc, w_main, w_small)


def _ssd_kernel(xs_ref, bm_ref, cm_ref, z_ref, dtr_ref, cwx_ref, cwb_ref, cwc_ref,
                cbx_ref, cbb_ref, cbc_ref, dtb_ref, alog_ref, dsk_ref, nw_ref,
                o_ref, ext_ref, state_ref):
    t = pl.program_id(2)
    q = xs_ref.shape[1]
    gw, ns = SSD_GW, SSD_STATE
    cw = gw + 2 * ns

    @pl.when(t == 0)
    def _():
        ext_ref[0:8, :] = jnp.zeros((8, cw), F32)
        state_ref[...] = jnp.zeros(state_ref.shape, F32)

    ext_ref[8:8 + q, 0:gw] = xs_ref[0].astype(F32)
    ext_ref[8:8 + q, gw:gw + ns] = bm_ref[0].astype(F32)
    ext_ref[8:8 + q, gw + ns:cw] = cm_ref[0].astype(F32)
    wts = jnp.concatenate([cwx_ref[...], cwb_ref[...], cwc_ref[...]], axis=1)
    acc = jnp.concatenate([cbx_ref[...], cbb_ref[...], cbc_ref[...]], axis=1)
    for j in range(SSD_CONV):
        acc = acc + wts[j:j + 1, :] * ext_ref[8 - (SSD_CONV - 1) + j:8 - (SSD_CONV - 1) + j + q, :]
    tail = ext_ref[q:q + 8, :]
    ext_ref[0:8, :] = tail
    xc = acc * _sigmoid(acc)
    xs = xc[:, 0:gw]
    bm = xc[:, gw:gw + ns].astype(BF16)
    cm = xc[:, gw + ns:cw].astype(BF16)
    xs_b = xs.astype(BF16)

    dtx = dtr_ref[0, 0] + dtb_ref[0]
    dt_row = jnp.maximum(dtx, 0.0) + jnp.log(1.0 + jnp.exp(-jnp.abs(dtx)))
    dta_row = dt_row * (-jnp.exp(alog_ref[0]))

    ti = lax.broadcasted_iota(jnp.int32, (q, q), 0)
    si = lax.broadcasted_iota(jnp.int32, (q, q), 1)
    tril = si <= ti
    eye = si == ti
    lane_head = lax.broadcasted_iota(jnp.int32, (1, gw), 1) // SSD_HEAD_DIM

    cb = lax.dot_general(cm, bm, (((1,), (1,)), ((), ())), preferred_element_type=F32)

    y = jnp.zeros((q, gw), F32)
    f_exp = jnp.zeros((q, gw), F32)
    f_w = jnp.zeros((q, gw), F32)
    e_dec = jnp.zeros((1, gw), F32)
    for r in range(SSD_R):
        dta_r = dta_row[r:r + 1, :]
        dt_r = dt_row[r:r + 1, :]
        acum_c = jnp.sum(jnp.where(tril, dta_r, 0.0), axis=1, keepdims=True)
        acum_r = jnp.sum(jnp.where(eye, acum_c, 0.0), axis=0, keepdims=True)
        dt_c = jnp.sum(jnp.where(eye, dt_r, 0.0), axis=1, keepdims=True)
        decay = jnp.exp(jnp.where(tril, acum_c - acum_r, -jnp.inf))
        m_r = (cb * decay * dt_r).astype(BF16)
        yd = jnp.dot(m_r, xs_b, preferred_element_type=F32)
        sel = lane_head == r
        y = y + jnp.where(sel, yd, 0.0)
        a_last = acum_r[:, q - 1:q]
        f_exp = f_exp + jnp.where(sel, jnp.exp(acum_c), 0.0)
        f_w = f_w + jnp.where(sel, dt_c * jnp.exp(a_last - acum_c), 0.0)
        e_dec = e_dec + jnp.where(sel, jnp.exp(a_last), 0.0)

    state = state_ref[...]
    y = y + jnp.dot(cm, state.astype(BF16), preferred_element_type=F32) * f_exp
    xw = (xs * f_w).astype(BF16)
    bm_t = jnp.transpose(xc[:, gw:gw + ns]).astype(BF16)
    state_ref[...] = state * e_dec + jnp.dot(bm_t, xw, preferred_element_type=F32)

    y = y + dsk_ref[...] * xs
    zf = z_ref[0].astype(F32)
    y = y * (zf * _sigmoid(zf))
    ms = jnp.mean(y * y, axis=-1, keepdims=True)
    o_ref[0] = (y * lax.rsqrt(ms + EPS) * nw_ref[...]).astype(BF16)


def _ssd(u, dt_rows, conv_w, conv_b, dt_bias, a_log, d_skip, norm_w):
    bsz, L, _ = u.shape
    q = min(SSD_Q, L)
    gw, ns, G = SSD_GW, SSD_STATE, SSD_GROUPS
    xb = OFF_XBC // gw
    bb = (OFF_XBC + SSD_D_INNER) // ns
    cb = bb + G
    cwb0 = SSD_D_INNER // ns
    return pl.pallas_call(
        _ssd_kernel,
        out_shape=jax.ShapeDtypeStruct((bsz, L, SSD_D_INNER), BF16),
        grid=(bsz, G, L // q),
        in_specs=[pl.BlockSpec((1, q, gw), lambda b, g, t: (b, t, xb + g)),
                  pl.BlockSpec((1, q, ns), lambda b, g, t: (b, t, bb + g)),
                  pl.BlockSpec((1, q, ns), lambda b, g, t: (b, t, cb + g)),
                  pl.BlockSpec((1, q, gw), lambda b, g, t: (b, t, g)),
                  pl.BlockSpec((1, 1, 8, q), lambda b, g, t: (b, g, 0, t)),
                  pl.BlockSpec((SSD_CONV, gw), lambda b, g, t: (0, g)),
                  pl.BlockSpec((SSD_CONV, ns), lambda b, g, t: (0, cwb0 + g)),
                  pl.BlockSpec((SSD_CONV, ns), lambda b, g, t: (0, cwb0 + G + g)),
                  pl.BlockSpec((1, gw), lambda b, g, t: (0, g)),
                  pl.BlockSpec((1, ns), lambda b, g, t: (0, cwb0 + g)),
                  pl.BlockSpec((1, ns), lambda b, g, t: (0, cwb0 + G + g)),
                  pl.BlockSpec((1, 8, 1), lambda b, g, t: (g, 0, 0)),
                  pl.BlockSpec((1, 8, 1), lambda b, g, t: (g, 0, 0)),
                  pl.BlockSpec((1, gw), lambda b, g, t: (0, g)),
                  pl.BlockSpec((1, gw), lambda b, g, t: (0, g))],
        out_specs=pl.BlockSpec((1, q, gw), lambda b, g, t: (b, t, g)),
        scratch_shapes=[pltpu.VMEM((q + 8, gw + 2 * ns), F32),
                        pltpu.VMEM((ns, gw), F32)],
        compiler_params=_cparams(("parallel", "parallel", "arbitrary")),
        name="ssd",
    )(u, u, u, u, dt_rows, conv_w, conv_w, conv_w, conv_b, conv_b, conv_b,
      dt_bias, a_log, d_skip, norm_w)


def _prep_kernel(q_ref, k_ref, v_ref, qi_ref, sm_ref, kpos_ref, qnw_ref, knw_ref, aug_ref,
                 qt_ref, ka_ref, vt_ref, qih_ref, kit_ref):
    tp = q_ref.shape[1]
    dh = ATT_HEAD_DIM
    qscale = (ATT_HEAD_DIM ** -0.5) * LOG2E

    def rms(xh, w):
        return xh * lax.rsqrt(jnp.mean(xh * xh, axis=-1, keepdims=True) + EPS) * w

    qf = q_ref[0].astype(F32)
    for h in range(ATT_HEADS):
        qh = rms(qf[:, h * dh:(h + 1) * dh], qnw_ref[...]) * qscale
        lanes = slice((h % 2) * tp, (h % 2 + 1) * tp)
        qt_ref[0, h // 2, 0:dh, lanes] = jnp.transpose(qh).astype(BF16)
        qt_ref[0, h // 2, dh:QK_AUG, lanes] = jnp.broadcast_to(aug_ref[h], (QK_AUG - dh, tp)).astype(BF16)

    kp = kpos_ref[0]
    lane = lax.broadcasted_iota(jnp.int32, (tp, QK_AUG - dh), 1)
    p_hi = (kp >> 7).astype(F32)
    p_lo = (kp & 127).astype(F32)
    pos_cols = jnp.where(lane < 3, p_hi, jnp.where(lane < 6, p_lo, 0.0)).astype(BF16)
    kf = k_ref[0].astype(F32)
    vf = v_ref[0].astype(F32)
    for g in range(ATT_KV_HEADS):
        ka_ref[0, g, :, 0:dh] = rms(kf[:, g * dh:(g + 1) * dh], knw_ref[...]).astype(BF16)
        ka_ref[0, g, :, dh:QK_AUG] = pos_cols
        vt_ref[0, g, 0:dh, :] = jnp.transpose(vf[:, g * dh:(g + 1) * dh]).astype(BF16)
        vt_ref[0, g, dh:VT_ROWS, :] = jnp.ones((VT_ROWS - dh, tp), BF16)

    qif = qi_ref[0].astype(F32)
    for h in range(IDX_HEADS):
        qh = qif[:, h * IDX_HEAD_DIM:(h + 1) * IDX_HEAD_DIM]
        qih_ref[0, :, h] = qh.reshape(tp // IDX_RB, IDX_RB, IDX_HEAD_DIM).astype(BF16)
    sm_t = jnp.transpose(sm_ref[0])
    for j in range(tp // IDX_KC):
        kit_ref[0, j] = sm_t[SM_KIDX:SM_KIDX + IDX_HEAD_DIM, j * IDX_KC:(j + 1) * IDX_KC].astype(BF16)


def _alibi_aug():
    out = np.zeros((ATT_HEADS, QK_AUG - ATT_HEAD_DIM, 1), np.float32)
    for h in range(ATT_HEADS):
        s = np.float32(2.0 ** (-ALIBI_MAX_BIAS * (h + 1) / ATT_HEADS)) * np.float32(LOG2E)
        rem = np.float32(s)
        for i in range(3):
            piece = np.float32(np.asarray(rem, np.float32).astype(BF16))
            out[h, i, 0] = piece * np.float32(128.0)
            out[h, 3 + i, 0] = piece
            rem = np.float32(rem - piece)
    return jnp.asarray(out)


def _alibi_corr():
    s = [np.float32(2.0) * np.float32(2.0 ** (-ALIBI_MAX_BIAS * (h + 1) / ATT_HEADS)) * np.float32(LOG2E)
         for h in range(ATT_HEADS)]
    return jnp.asarray(np.array(s, np.float32).reshape(ATT_HEADS, 1, 1))


def _dsa_prep(u, small, kpos_col, q_norm, k_norm):
    bsz, L, _ = u.shape
    tp = min(PREP_T, L)
    nkc = L // IDX_KC
    return pl.pallas_call(
        _prep_kernel,
        out_shape=(jax.ShapeDtypeStruct((bsz, ATT_HEADS // 2, QK_AUG, 2 * L), BF16),
                   jax.ShapeDtypeStruct((bsz, ATT_KV_HEADS, L, QK_AUG), BF16),
                   jax.ShapeDtypeStruct((bsz, ATT_KV_HEADS, VT_ROWS, L), BF16),
                   jax.ShapeDtypeStruct((bsz, L // IDX_RB, IDX_HEADS, IDX_RB, IDX_HEAD_DIM), BF16),
                   jax.ShapeDtypeStruct((bsz, nkc, IDX_HEAD_DIM, IDX_KC), BF16)),
        grid=(bsz, L // tp),
        in_specs=[pl.BlockSpec((1, tp, ATT_WIDTH), lambda b, i: (b, i, OFF_Q // ATT_WIDTH)),
                  pl.BlockSpec((1, tp, ATT_KV_WIDTH), lambda b, i: (b, i, OFF_K // ATT_KV_WIDTH)),
                  pl.BlockSpec((1, tp, ATT_KV_WIDTH), lambda b, i: (b, i, OFF_V // ATT_KV_WIDTH)),
                  pl.BlockSpec((1, tp, IDX_HEADS * IDX_HEAD_DIM),
                               lambda b, i: (b, i, OFF_QI // (IDX_HEADS * IDX_HEAD_DIM))),
                  pl.BlockSpec((1, tp, LANES), lambda b, i: (b, i, 0)),
                  pl.BlockSpec((1, tp, 1), lambda b, i: (b, i, 0)),
                  pl.BlockSpec((1, ATT_HEAD_DIM), lambda b, i: (0, 0)),
                  pl.BlockSpec((1, ATT_HEAD_DIM), lambda b, i: (0, 0)),
                  pl.BlockSpec((ATT_HEADS, QK_AUG - ATT_HEAD_DIM, 1), lambda b, i: (0, 0, 0))],
        out_specs=(pl.BlockSpec((1, ATT_HEADS // 2, QK_AUG, 2 * tp), lambda b, i: (b, 0, 0, i)),
                   pl.BlockSpec((1, ATT_KV_HEADS, tp, QK_AUG), lambda b, i: (b, 0, i, 0)),
                   pl.BlockSpec((1, ATT_KV_HEADS, VT_ROWS, tp), lambda b, i: (b, 0, 0, i)),
                   pl.BlockSpec((1, tp // IDX_RB, IDX_HEADS, IDX_RB, IDX_HEAD_DIM),
                                lambda b, i: (b, i, 0, 0, 0)),
                   pl.BlockSpec((1, tp // IDX_KC, IDX_HEAD_DIM, IDX_KC), lambda b, i: (b, i, 0, 0))),
        compiler_params=_cparams(("parallel", "parallel")),
        name="dsa_prep",
    )(u, u, u, u, small, kpos_col, q_norm.reshape(1, -1), k_norm.reshape(1, -1), _alibi_aug())


_INT_MIN = -2 ** 31
_NEG_INF_KEY = -2139095041


def _tree_sum(terms):
    while len(terms) > 1:
        terms = [terms[i] + terms[i + 1] for i in range(0, len(terms) - 1, 2)] + (
            [terms[-1]] if len(terms) % 2 else [])
    return terms[0]


def _indexer_kernel(qih_ref, kit_ref, sm_ref, qpos_ref, kposr_ref, o_ref,
                    keys_ref, wb_ref, bmax_ref, *, top_k):
    qi = pl.program_id(1)
    nrb = qih_ref.shape[1]
    tq = nrb * IDX_RB
    kc_w = kit_ref.shape[3]
    nkc_all = kit_ref.shape[1]
    hw = kc_w // 2
    assert hw == LANES and top_k <= kc_w
    n_kc = jnp.minimum(((qi + 1) * tq + kc_w - 1) // kc_w, nkc_all)
    idx_scale = (IDX_HEAD_DIM * IDX_HEADS) ** -0.5

    sm = sm_ref[0]
    for h in range(IDX_HEADS):
        wb_ref[h] = jnp.broadcast_to(sm[:, SM_WIDX + h:SM_WIDX + h + 1], (tq, LANES))
    bmax_ref[...] = jnp.full(bmax_ref.shape, _INT_MIN, jnp.int32)

    def chunk_body(kc, carry):
        kt = kit_ref[0, kc]
        kchunk = kposr_ref[0, kc] >> 6

        def qk(rb):
            return jnp.dot(qih_ref[0, rb].reshape(IDX_HEADS * IDX_RB, IDX_HEAD_DIM), kt,
                           preferred_element_type=F32)

        s_next = qk(0)
        for rb in range(nrb):
            rows = slice(rb * IDX_RB, (rb + 1) * IDX_RB)
            s = s_next
            if rb + 1 < nrb:
                s_next = qk(rb + 1)
            t0, t1 = [], []
            for h in range(IDX_HEADS):
                rel = jnp.maximum(s[h * IDX_RB:(h + 1) * IDX_RB, :], 0.0)
                w = wb_ref[h, rows, :]
                t0.append(w * rel[:, 0:hw])
                t1.append(w * rel[:, hw:kc_w])
            score = jnp.concatenate([_tree_sum(t0), _tree_sum(t1)], axis=1) * idx_scale
            qchunk = qpos_ref[0, rows, :] >> 6
            score = jnp.where(kchunk <= qchunk, score, -jnp.inf)
            bits = pltpu.bitcast(score, jnp.int32)
            key = bits ^ ((bits >> 31) & 0x7FFFFFFF)
            keys_ref[kc, rows, :] = key
            bmax_ref[rows, :] = jnp.maximum(bmax_ref[rows, :], key)
        return carry

    lax.fori_loop(0, n_kc, chunk_body, 0)
    keys_ref[n_kc] = jnp.full((tq, kc_w), _INT_MIN, jnp.int32)

    bm = bmax_ref[...]
    lb = jnp.min(bm, axis=1, keepdims=True)
    ub = jnp.max(bm, axis=1, keepdims=True)
    span = ub - lb
    wrapped = span < 0
    lb = jnp.where(wrapped, _INT_MIN, lb)
    nb = jnp.where(wrapped, 32, 32 - lax.clz(span))
    nbits = jnp.max(nb)

    def bit_body(i, thr):
        cand = thr + jnp.left_shift(jnp.int32(1), nbits - 1 - i)
        candb = jnp.broadcast_to(cand, (tq, hw))

        def cnt_body(pair, acc):
            for kc in (2 * pair, 2 * pair + 1):
                k = keys_ref[kc]
                acc = acc + jnp.where(k[:, 0:hw] >= candb, 1, 0) + jnp.where(k[:, hw:kc_w] >= candb, 1, 0)
            return acc

        acc = lax.fori_loop(0, (n_kc + 1) // 2, cnt_body, jnp.zeros((tq, hw), jnp.int32))
        cnt = jnp.sum(acc, axis=1, keepdims=True)
        return jnp.where(jnp.logical_and(cnt >= top_k, cand > thr), cand, thr)

    thr = lax.fori_loop(0, nbits, bit_body, lb)
    thr = jnp.maximum(thr, _NEG_INF_KEY + 1)
    thr_b = jnp.broadcast_to(thr, (tq, hw))
    thr_full = jnp.concatenate([thr_b] * (kc_w // hw), axis=1)

    def mask_body(kc, carry):
        m = jnp.where(keys_ref[kc] >= thr_full, 0.0, NEG_BIG).astype(BF16)
        r0 = pl.multiple_of(kc * kc_w, kc_w)
        o_ref[0, pl.ds(r0, kc_w), :] = jnp.transpose(m)
        return carry

    lax.fori_loop(0, n_kc, mask_body, 0)

    def fill_body(kc, carry):
        r0 = pl.multiple_of(kc * kc_w, kc_w)
        o_ref[0, pl.ds(r0, kc_w), :] = jnp.full((kc_w, tq), NEG_BIG, BF16)
        return carry

    lax.fori_loop(n_kc, nkc_all, fill_body, 0)


def _indexer(qih, kit, small, qpos_col, kpos_rows, top_k):
    bsz, L = small.shape[0], small.shape[1]
    tq = min(IDX_TQ, L)
    nkc = L // IDX_KC
    return pl.pallas_call(
        functools.partial(_indexer_kernel, top_k=top_k),
        out_shape=jax.ShapeDtypeStruct((bsz, L, L), BF16),
        grid=(bsz, L // tq),
        in_specs=[pl.BlockSpec((1, tq // IDX_RB, IDX_HEADS, IDX_RB, IDX_HEAD_DIM),
                               lambda b, i: (b, i, 0, 0, 0)),
                  pl.BlockSpec((1, nkc, IDX_HEAD_DIM, IDX_KC), lambda b, i: (b, 0, 0, 0)),
                  pl.BlockSpec((1, tq, LANES), lambda b, i: (b, i, 0)),
                  pl.BlockSpec((1, tq, 1), lambda b, i: (b, i, 0)),
                  pl.BlockSpec((1, nkc, 1, IDX_KC), lambda b, i: (b, 0, 0, 0))],
        out_specs=pl.BlockSpec((1, L, tq), lambda b, i: (b, 0, i)),
        scratch_shapes=[pltpu.VMEM((nkc + 1, tq, IDX_KC), jnp.int32),
                        pltpu.VMEM((IDX_HEADS, tq, LANES), F32),
                        pltpu.VMEM((tq, IDX_KC), jnp.int32)],
        compiler_params=_cparams(("parallel", "arbitrary")),
        name="indexer",
    )(qih, kit, small, qpos_col, kpos_rows)


def _attn_kernel(qtab_ref, ktab_ref, qt_ref, ka_ref, vt_ref, mk_ref, kpos_ref, qpos_ref, corr_ref,
                 o_ref, bias_ref, s_ref, smax_ref, *state_refs):
    s_id = pl.program_id(1)
    qi = qtab_ref[s_id]
    ki = ktab_ref[s_id]
    tq = qt_ref.shape[3] // 2
    tk = ka_ref.shape[2]
    dh = ATT_HEAD_DIM
    n_pairs = ATT_HEADS // 2
    kmax = ((qi + 1) * tq - 1) // tk
    m_refs, acc_refs = state_refs[:n_pairs], state_refs[n_pairs:]

    @pl.when(ki == 0)
    def _():
        for j in range(n_pairs):
            m_refs[j][...] = jnp.full(m_refs[j].shape, -jnp.inf, F32)
            acc_refs[j][...] = jnp.zeros(acc_refs[j].shape, F32)

    def step(diag):
        bias = mk_ref[0].astype(F32)
        bias_ref[:, 0:tq] = bias
        bias_ref[:, tq:2 * tq] = bias
        if diag:
            later = jnp.maximum(kpos_ref[0][:, 0:1] - qpos_ref[0], 0).astype(F32)
            later = jnp.concatenate([later, later], axis=1)

        def scores(j, slot):
            s = jnp.dot(ka_ref[0, (2 * j) // ATT_RQ], qt_ref[0, j],
                        preferred_element_type=F32) + bias_ref[...]
            if diag:
                s = s - corr_ref[j] * later
            s_ref[slot] = s
            smax_ref[slot] = jnp.max(s, axis=0, keepdims=True)

        for j in range(ATT_LOOKAHEAD):
            scores(j, j)
        for j in range(n_pairs):
            if j + ATT_LOOKAHEAD < n_pairs:
                scores(j + ATT_LOOKAHEAD, (j + ATT_LOOKAHEAD) % ATT_SLOTS)
            m_old = m_refs[j][...]
            m_new = jnp.maximum(m_old, smax_ref[j % ATT_SLOTS])
            p = jnp.exp2(s_ref[j % ATT_SLOTS] - m_new).astype(BF16)
            acc_refs[j][...] = acc_refs[j][...] * jnp.exp2(m_old - m_new) + jnp.dot(
                vt_ref[0, (2 * j) // ATT_RQ], p, preferred_element_type=F32)
            m_refs[j][...] = m_new

    is_diag = (ki + 1) * tk - 1 > qi * tq

    @pl.when(is_diag)
    def _():
        step(True)

    @pl.when(jnp.logical_not(is_diag))
    def _():
        step(False)

    @pl.when(ki == kmax)
    def _():
        for h in range(ATT_HEADS):
            lanes = slice((h % 2) * tq, (h % 2 + 1) * tq)
            out = acc_refs[h // 2][0:dh, lanes] / acc_refs[h // 2][dh:dh + 1, lanes]
            o_ref[0, :, h * dh:(h + 1) * dh] = jnp.transpose(out).astype(BF16)


def _attn(qt, ka, vt, mask_t, kpos_col, qpos_row):
    bsz, L = mask_t.shape[0], mask_t.shape[1]
    tq, tk = min(ATT_TQ, L), min(ATT_TK, L)
    assert qt.shape[3] == 2 * L and min(PREP_T, L) == tq
    n_pairs = ATT_HEADS // 2
    pairs = [(i, k) for i in range(L // tq) for k in range(((i + 1) * tq - 1) // tk + 1)]
    qtab = jnp.asarray(np.array([p[0] for p in pairs], np.int32))
    ktab = jnp.asarray(np.array([p[1] for p in pairs], np.int32))
    corr = jnp.broadcast_to(_alibi_corr().reshape(n_pairs, 1, 2, 1), (n_pairs, 1, 2, tq)).reshape(
        n_pairs, 1, 2 * tq)
    grid_spec = pltpu.PrefetchScalarGridSpec(
        num_scalar_prefetch=2,
        grid=(bsz, len(pairs)),
        in_specs=[pl.BlockSpec((1, n_pairs, QK_AUG, 2 * tq), lambda b, s, qt_, kt_: (b, 0, 0, qt_[s])),
                  pl.BlockSpec((1, ATT_KV_HEADS, tk, QK_AUG), lambda b, s, qt_, kt_: (b, 0, kt_[s], 0)),
                  pl.BlockSpec((1, ATT_KV_HEADS, VT_ROWS, tk), lambda b, s, qt_, kt_: (b, 0, 0, kt_[s])),
                  pl.BlockSpec((1, tk, tq), lambda b, s, qt_, kt_: (b, kt_[s], qt_[s])),
                  pl.BlockSpec((1, tk, LANES), lambda b, s, qt_, kt_: (b, kt_[s], 0)),
                  pl.BlockSpec((1, 1, tq), lambda b, s, qt_, kt_: (b, 0, qt_[s])),
                  pl.BlockSpec((n_pairs, 1, 2 * tq), lambda b, s, qt_, kt_: (0, 0, 0))],
        out_specs=pl.BlockSpec((1, tq, ATT_WIDTH), lambda b, s, qt_, kt_: (b, qt_[s], 0)),
        scratch_shapes=([pltpu.VMEM((tk, 2 * tq), F32), pltpu.VMEM((ATT_SLOTS, tk, 2 * tq), F32),
                         pltpu.VMEM((ATT_SLOTS, 1, 2 * tq), F32)]
                        + [pltpu.VMEM((1, 2 * tq), F32) for _ in range(n_pairs)]
                        + [pltpu.VMEM((VT_ROWS, 2 * tq), F32) for _ in range(n_pairs)]))
    return pl.pallas_call(
        _attn_kernel,
        out_shape=jax.ShapeDtypeStruct((bsz, L, ATT_WIDTH), BF16),
        grid_spec=grid_spec,
        compiler_params=_cparams(("parallel", "arbitrary")),
        name="attn",
    )(qtab, ktab, qt, ka, vt, mask_t, kpos_col, qpos_row, corr)


def _merge_kernel(ys_ref, ya_ref, g0_ref, g1_ref, ws_ref, wa_ref, o_ref):
    ps = jnp.dot(ys_ref[0], ws_ref[...], preferred_element_type=F32)
    pa = jnp.dot(ya_ref[0], wa_ref[...], preferred_element_type=F32)
    g0 = _sigmoid(g0_ref[0].astype(F32))
    g1 = _sigmoid(g1_ref[0].astype(F32))
    o_ref[0] = (g0 * ps + g1 * pa).astype(BF16)


def _merge(y_ssd, y_att, u, w_s, w_a):
    bsz, L, _ = y_ssd.shape
    d = w_s.shape[1]
    tm, tn = min(MRG_TM, L), min(MRG_TN, d)
    g0b = OFF_GATE // tn
    g1b = (OFF_GATE + d) // tn
    return pl.pallas_call(
        _merge_kernel,
        out_shape=jax.ShapeDtypeStruct((bsz, L, d), BF16),
        grid=(bsz, L // tm, d // tn),
        in_specs=[pl.BlockSpec((1, tm, SSD_D_INNER), lambda b, i, j: (b, i, 0)),
                  pl.BlockSpec((1, tm, ATT_WIDTH), lambda b, i, j: (b, i, 0)),
                  pl.BlockSpec((1, tm, tn), lambda b, i, j: (b, i, g0b + j)),
                  pl.BlockSpec((1, tm, tn), lambda b, i, j: (b, i, g1b + j)),
                  pl.BlockSpec((SSD_D_INNER, tn), lambda b, i, j: (0, j)),
                  pl.BlockSpec((ATT_WIDTH, tn), lambda b, i, j: (0, j))],
        out_specs=pl.BlockSpec((1, tm, tn), lambda b, i, j: (b, i, j)),
        compiler_params=_cparams(("parallel", "parallel", "arbitrary")),
        name="merge",
    )(y_ssd, y_att, u, u, w_s, w_a)


def _outproj_kernel(m_ref, h_ref, g_ref, w_ref, o_ref):
    o_ref[0] = h_ref[0] + g_ref[0] * jnp.dot(m_ref[0], w_ref[...], preferred_element_type=F32)


def _outproj(merged, h, gate, w_out):
    bsz, L, d = h.shape
    tm, tn = min(MRG_TM, L), min(MRG_TN, d)
    return pl.pallas_call(
        _outproj_kernel,
        out_shape=jax.ShapeDtypeStruct((bsz, L, d), F32),
        grid=(bsz, L // tm, d // tn),
        in_specs=[pl.BlockSpec((1, tm, d), lambda b, i, j: (b, i, 0)),
                  pl.BlockSpec((1, tm, tn), lambda b, i, j: (b, i, j)),
                  pl.BlockSpec((1, 1, tn), lambda b, i, j: (b, 0, j)),
                  pl.BlockSpec((d, tn), lambda b, i, j: (0, j))],
        out_specs=pl.BlockSpec((1, tm, tn), lambda b, i, j: (b, i, j)),
        compiler_params=_cparams(("parallel", "parallel", "arbitrary")),
        name="outproj",
    )(merged, h, gate, w_out)


def _split_w_in(w_in, d):
    sizes = (SSD_D_INNER, SSD_D_INNER + 2 * SSD_GROUPS * SSD_STATE, SSD_HEADS, ATT_WIDTH, ATT_KV_WIDTH,
             ATT_KV_WIDTH, IDX_HEADS * IDX_HEAD_DIM, IDX_HEAD_DIM, IDX_HEADS, 2 * d)
    offs = np.concatenate([[0], np.cumsum(sizes)])
    seg = [w_in[:, int(offs[i]):int(offs[i + 1])] for i in range(len(sizes))]
    w_z, w_xbc, w_dt, w_q, w_k, w_v, w_qi, w_ki, w_wi, w_g = seg
    w_main = jnp.concatenate([w_z, w_xbc, w_q, w_k, w_v, w_qi, w_g], axis=1).astype(BF16)
    pad = jnp.zeros((w_in.shape[0], LANES - (IDX_HEAD_DIM + IDX_HEADS + SSD_HEADS)), w_in.dtype)
    w_small = jnp.concatenate([w_ki, w_wi, w_dt, pad], axis=1).astype(BF16)
    return w_main, w_small


def kernel(x, c, positions, w_ada, b_ada, norm_ffn1, ffn1_w1, ffn1_w3, ffn1_w2, norm_mix, w_in,
           ssd_conv_w, ssd_conv_b, ssd_dt_bias, ssd_a_log, ssd_d, ssd_norm, q_norm, k_norm,
           w_br_ssd, w_br_att, w_out, norm_ffn2, ffn2_w1, ffn2_w3, ffn2_w2):
    bsz, L, d = x.shape
    depth = w_ada.shape[0]
    top_k = min(TOPK_MAX, L // 4)
    G, R = SSD_GROUPS, SSD_R
    kpos_col = positions.reshape(bsz, L, 1)
    kpos_lanes = jnp.broadcast_to(kpos_col, (bsz, L, LANES))
    qpos_row = positions.reshape(bsz, 1, L)
    kpos_rows = positions.reshape(bsz, L // IDX_KC, 1, IDX_KC)

    def pad_rows(a):
        return jnp.pad(a.reshape(G, R), ((0, 0), (0, 8 - R))).reshape(G, 8, 1)

    h = x
    for l in range(depth):
        mod = _ada(c, w_ada[l], b_ada[l]).reshape(bsz, N_MOD, 1, d)
        sh1, sc1, g1, sh2, sc2, g2, sh3, sc3, g3 = [mod[:, i] for i in range(N_MOD)]

        h = _ffn(h, norm_ffn1[l].reshape(1, d), sh1, sc1, g1,
                 ffn1_w1[l].astype(BF16), ffn1_w3[l].astype(BF16), ffn1_w2[l].astype(BF16))

        w_main, w_small = _split_w_in(w_in[l], d)
        u, small = _inproj(h, norm_mix[l].reshape(1, d), sh2, sc2, w_main, w_small)

        dt_raw = small[:, :, SM_DT:SM_DT + SSD_HEADS].reshape(bsz, L, G, R)
        dt_rows = jnp.pad(jnp.transpose(dt_raw, (0, 2, 3, 1)), ((0, 0), (0, 0), (0, 8 - R), (0, 0)))
        y_ssd = _ssd(u, dt_rows, ssd_conv_w[l], ssd_conv_b[l].reshape(1, -1),
                     pad_rows(ssd_dt_bias[l]), pad_rows(ssd_a_log[l]),
                     jnp.repeat(ssd_d[l], SSD_HEAD_DIM).reshape(1, SSD_D_INNER),
                     ssd_norm[l].reshape(1, SSD_D_INNER))

        qt, ka, vt, qih, kit = _dsa_prep(u, small, kpos_col, q_norm[l], k_norm[l])
        mask_t = _indexer(qih, kit, small, kpos_col, kpos_rows, top_k)
        y_att = _attn(qt, ka, vt, mask_t, kpos_lanes, qpos_row)

        merged = _merge(y_ssd, y_att, u, w_br_ssd[l].astype(BF16), w_br_att[l].astype(BF16))
        h = _outproj(merged, h, g2, w_out[l].astype(BF16))

        h = _ffn(h, norm_ffn2[l].reshape(1, d), sh3, sc3, g3,
                 ffn2_w1[l].astype(BF16), ffn2_w3[l].astype(BF16), ffn2_w2[l].astype(BF16))
    return h
```

```python
import functools
import math

import numpy as np
import jax
import jax.numpy as jnp
from jax import lax
from jax.experimental import pallas as pl
from jax.experimental.pallas import tpu as pltpu

F32 = jnp.float32
BF16 = jnp.bfloat16

CHUNK = 64
EPS = 1e-6
N_MOD = 9
SSD_D_INNER = 2048
SSD_HEAD_DIM = 64
SSD_GROUPS = 8
SSD_HEADS = SSD_D_INNER // SSD_HEAD_DIM
SSD_R = SSD_HEADS // SSD_GROUPS
SSD_GW = SSD_D_INNER // SSD_GROUPS
SSD_STATE = 128
SSD_CONV = 4
ATT_HEADS = 16
ATT_KV_HEADS = 4
ATT_RQ = ATT_HEADS // ATT_KV_HEADS
ATT_HEAD_DIM = 128
ATT_WIDTH = ATT_HEADS * ATT_HEAD_DIM
ATT_KV_WIDTH = ATT_KV_HEADS * ATT_HEAD_DIM
IDX_HEADS = 16
IDX_HEAD_DIM = 64
TOPK_MAX = 256
ALIBI_MAX_BIAS = 8.0
LOG2E = 1.4426950408889634
NEG_BIG = -1e30

LANES = 128
QK_AUG = 256
VT_ROWS = ATT_HEAD_DIM + 16
VMEM_LIMIT = 56 * 1024 * 1024

OFF_Z = 0
OFF_XBC = OFF_Z + SSD_D_INNER
OFF_Q = OFF_XBC + SSD_D_INNER + 2 * SSD_GROUPS * SSD_STATE
OFF_K = OFF_Q + ATT_WIDTH
OFF_V = OFF_K + ATT_KV_WIDTH
OFF_QI = OFF_V + ATT_KV_WIDTH
OFF_GATE = OFF_QI + IDX_HEADS * IDX_HEAD_DIM
SM_KIDX = 0
SM_WIDX = IDX_HEAD_DIM
SM_DT = SM_WIDX + IDX_HEADS

FFN_TM, FFN_TF = 512, 512
NORM_RB = 128
INP_TM, INP_TN = 1024, 1024
SSD_Q = 256
PREP_T = 256
IDX_TQ, IDX_KC, IDX_RB = 128, 256, 16
ATT_TQ, ATT_TK = 256, 512
ATT_LOOKAHEAD = 2
ATT_SLOTS = ATT_LOOKAHEAD + 1
MRG_TM, MRG_TN = 1024, 512


def _cparams(sem):
    return pltpu.CompilerParams(dimension_semantics=sem, vmem_limit_bytes=VMEM_LIMIT)


def _sigmoid(x):
    return 1.0 / (1.0 + jnp.exp(-x))


def _rms_mod(x, nw, sh, sc):
    ms = jnp.mean(x * x, axis=-1, keepdims=True)
    return (x * lax.rsqrt(ms + EPS) * nw) * (1.0 + sc) + sh


def _rms_mod_rows(x_ref, nw_ref, sh_ref, sc_ref, hn_ref):
    tm = hn_ref.shape[0]
    rb = min(NORM_RB, tm)

    def body(i, carry):
        rows = pl.ds(pl.multiple_of(i * rb, rb), rb)
        hn_ref[rows, :] = _rms_mod(x_ref[0, rows, :], nw_ref[...], sh_ref[0], sc_ref[0]).astype(BF16)
        return carry

    lax.fori_loop(0, tm // rb, body, 0)


def _ada_kernel(ct_ref, w_ref, b_ref, o_ref):
    ct = ct_ref[...]
    ca = ct * _sigmoid(ct)
    w = w_ref[...]
    rows = [jnp.sum(w * ca[:, b:b + 1], axis=0, keepdims=True) for b in range(ct.shape[1])]
    o_ref[...] = jnp.concatenate(rows, axis=0) + b_ref[...]


def _ada(c, w_ada, b_ada):
    bsz, d = c.shape
    n = w_ada.shape[1]
    tn = 1024 if n % 1024 == 0 else n
    return pl.pallas_call(
        _ada_kernel,
        out_shape=jax.ShapeDtypeStruct((bsz, n), F32),
        grid=(n // tn,),
        in_specs=[pl.BlockSpec((d, bsz), lambda j: (0, 0)),
                  pl.BlockSpec((d, tn), lambda j: (0, j)),
                  pl.BlockSpec((1, tn), lambda j: (0, j))],
        out_specs=pl.BlockSpec((bsz, tn), lambda j: (0, j)),
        compiler_params=_cparams(("arbitrary",)),
        name="ada",
    )(c.T, w_ada, b_ada.reshape(1, n))


def _ffn_kernel(x_ref, nw_ref, sh_ref, sc_ref, g_ref, w1_ref, w3_ref, w2_ref, o_ref, hn_ref):
    f = pl.program_id(2)

    @pl.when(f == 0)
    def _():
        _rms_mod_rows(x_ref, nw_ref, sh_ref, sc_ref, hn_ref)
        o_ref[0] = jnp.zeros(o_ref.shape[1:], F32)

    hn = hn_ref[...]
    a = jnp.dot(hn, w1_ref[...], preferred_element_type=F32)
    b = jnp.dot(hn, w3_ref[...], preferred_element_type=F32)
    g = (a * _sigmoid(a) * b).astype(BF16)
    o_ref[0] += jnp.dot(g, w2_ref[...], preferred_element_type=F32)

    @pl.when(f == pl.num_programs(2) - 1)
    def _():
        o_ref[0] = x_ref[0] + 0.5 * g_ref[0] * o_ref[0]


def _ffn(h, nw, sh, sc, gate, w1, w3, w2):
    bsz, L, d = h.shape
    ff = w1.shape[1]
    tm, tf = min(FFN_TM, L), min(FFN_TF, ff)
    vec = pl.BlockSpec((1, 1, d), lambda b, i, f: (b, 0, 0))
    return pl.pallas_call(
        _ffn_kernel,
        out_shape=jax.ShapeDtypeStruct((bsz, L, d), F32),
        grid=(bsz, L // tm, ff // tf),
        in_specs=[pl.BlockSpec((1, tm, d), lambda b, i, f: (b, i, 0)),
                  pl.BlockSpec((1, d), lambda b, i, f: (0, 0)),
                  vec, vec, vec,
                  pl.BlockSpec((d, tf), lambda b, i, f: (0, f)),
                  pl.BlockSpec((d, tf), lambda b, i, f: (0, f)),
                  pl.BlockSpec((tf, d), lambda b, i, f: (f, 0))],
        out_specs=pl.BlockSpec((1, tm, d), lambda b, i, f: (b, i, 0)),
        scratch_shapes=[pltpu.VMEM((tm, d), BF16)],
        compiler_params=_cparams(("parallel", "parallel", "arbitrary")),
        name="ffn",
    )(h, nw, sh, sc, gate, w1, w3, w2)


def _inproj_kernel(x_ref, nw_ref, sh_ref, sc_ref, w_ref, ws_ref, u_ref, s_ref, hn_ref):
    j = pl.program_id(2)

    @pl.when(j == 0)
    def _():
        _rms_mod_rows(x_ref, nw_ref, sh_ref, sc_ref, hn_ref)
        s_ref[0] = jnp.dot(hn_ref[...], ws_ref[...], preferred_element_type=F32)

    u_ref[0] = jnp.dot(hn_ref[...], w_ref[...], preferred_element_type=F32).astype(BF16)


def _inproj(h, nw, sh, sc, w_main, w_small):
    bsz, L, d = h.shape
    n = w_main.shape[1]
    tm = min(INP_TM, L)
    tn = INP_TN if n % INP_TN == 0 else INP_TN // 2
    vec = pl.BlockSpec((1, 1, d), lambda b, i, j: (b, 0, 0))
    return pl.pallas_call(
        _inproj_kernel,
        out_shape=(jax.ShapeDtypeStruct((bsz, L, n), BF16),
                   jax.ShapeDtypeStruct((bsz, L, LANES), F32)),
        grid=(bsz, L // tm, n // tn),
        in_specs=[pl.BlockSpec((1, tm, d), lambda b, i, j: (b, i, 0)),
                  pl.BlockSpec((1, d), lambda b, i, j: (0, 0)),
                  vec, vec,
                  pl.BlockSpec((d, tn), lambda b, i, j: (0, j)),
                  pl.BlockSpec((d, LANES), lambda b, i, j: (0, 0))],
        out_specs=(pl.BlockSpec((1, tm, tn), lambda b, i, j: (b, i, j)),
                   pl.BlockSpec((1, tm, LANES), lambda b, i, j: (b, i, 0))),
        scratch_shapes=[pltpu.VMEM((tm, d), BF16)],
        compiler_params=_cparams(("parallel", "parallel", "arbitrary")),
        name="inproj",
    )(h, nw, sh, sc, w_main, w_small)


def _ssd_kernel(xs_ref, bm_ref, cm_ref, z_ref, dtr_ref, cwx_ref, cwb_ref, cwc_ref,
                cbx_ref, cbb_ref, cbc_ref, dtb_ref, alog_ref, dsk_ref, nw_ref,
                o_ref, ext_ref, state_ref):
    t = pl.program_id(2)
    q = xs_ref.shape[1]
    gw, ns = SSD_GW, SSD_STATE
    cw = gw + 2 * ns

    @pl.when(t == 0)
    def _():
        ext_ref[0:8, :] = jnp.zeros((8, cw), F32)
        state_ref[...] = jnp.zeros(state_ref.shape, F32)

    ext_ref[8:8 + q, 0:gw] = xs_ref[0].astype(F32)
    ext_ref[8:8 + q, gw:gw + ns] = bm_ref[0].astype(F32)
    ext_ref[8:8 + q, gw + ns:cw] = cm_ref[0].astype(F32)
    wts = jnp.concatenate([cwx_ref[...], cwb_ref[...], cwc_ref[...]], axis=1)
    acc = jnp.concatenate([cbx_ref[...], cbb_ref[...], cbc_ref[...]], axis=1)
    for j in range(SSD_CONV):
        acc = acc + wts[j:j + 1, :] * ext_ref[8 - (SSD_CONV - 1) + j:8 - (SSD_CONV - 1) + j + q, :]
    tail = ext_ref[q:q + 8, :]
    ext_ref[0:8, :] = tail
    xc = acc * _sigmoid(acc)
    xs = xc[:, 0:gw]
    bm = xc[:, gw:gw + ns].astype(BF16)
    cm = xc[:, gw + ns:cw].astype(BF16)
    xs_b = xs.astype(BF16)

    dtx = dtr_ref[0, 0] + dtb_ref[0]
    dt_row = jnp.maximum(dtx, 0.0) + jnp.log(1.0 + jnp.exp(-jnp.abs(dtx)))
    dta_row = dt_row * (-jnp.exp(alog_ref[0]))

    ti = lax.broadcasted_iota(jnp.int32, (q, q), 0)
    si = lax.broadcasted_iota(jnp.int32, (q, q), 1)
    tril = si <= ti
    eye = si == ti
    lane_head = lax.broadcasted_iota(jnp.int32, (1, gw), 1) // SSD_HEAD_DIM

    cb = lax.dot_general(cm, bm, (((1,), (1,)), ((), ())), preferred_element_type=F32)

    y = jnp.zeros((q, gw), F32)
    f_exp = jnp.zeros((q, gw), F32)
    f_w = jnp.zeros((q, gw), F32)
    e_dec = jnp.zeros((1, gw), F32)
    for r in range(SSD_R):
        dta_r = dta_row[r:r + 1, :]
        dt_r = dt_row[r:r + 1, :]
        acum_c = jnp.sum(jnp.where(tril, dta_r, 0.0), axis=1, keepdims=True)
        acum_r = jnp.sum(jnp.where(eye, acum_c, 0.0), axis=0, keepdims=True)
        dt_c = jnp.sum(jnp.where(eye, dt_r, 0.0), axis=1, keepdims=True)
        decay = jnp.exp(jnp.where(tril, acum_c - acum_r, -jnp.inf))
        m_r = (cb * decay * dt_r).astype(BF16)
        yd = jnp.dot(m_r, xs_b, preferred_element_type=F32)
        sel = lane_head == r
        y = y + jnp.where(sel, yd, 0.0)
        a_last = acum_r[:, q - 1:q]
        f_exp = f_exp + jnp.where(sel, jnp.exp(acum_c), 0.0)
        f_w = f_w + jnp.where(sel, dt_c * jnp.exp(a_last - acum_c), 0.0)
        e_dec = e_dec + jnp.where(sel, jnp.exp(a_last), 0.0)

    state = state_ref[...]
    y = y + jnp.dot(cm, state.astype(BF16), preferred_element_type=F32) * f_exp
    xw = (xs * f_w).astype(BF16)
    bm_t = jnp.transpose(xc[:, gw:gw + ns]).astype(BF16)
    state_ref[...] = state * e_dec + jnp.dot(bm_t, xw, preferred_element_type=F32)

    y = y + dsk_ref[...] * xs
    zf = z_ref[0].astype(F32)
    y = y * (zf * _sigmoid(zf))
    ms = jnp.mean(y * y, axis=-1, keepdims=True)
    o_ref[0] = (y * lax.rsqrt(ms + EPS) * nw_ref[...]).astype(BF16)


def _ssd(u, dt_rows, conv_w, conv_b, dt_bias, a_log, d_skip, norm_w):
    bsz, L, _ = u.shape
    q = min(SSD_Q, L)
    gw, ns, G = SSD_GW, SSD_STATE, SSD_GROUPS
    xb = OFF_XBC // gw
    bb = (OFF_XBC + SSD_D_INNER) // ns
    cb = bb + G
    cwb0 = SSD_D_INNER // ns
    return pl.pallas_call(
        _ssd_kernel,
        out_shape=jax.ShapeDtypeStruct((bsz, L, SSD_D_INNER), BF16),
        grid=(bsz, G, L // q),
        in_specs=[pl.BlockSpec((1, q, gw), lambda b, g, t: (b, t, xb + g)),
                  pl.BlockSpec((1, q, ns), lambda b, g, t: (b, t, bb + g)),
                  pl.BlockSpec((1, q, ns), lambda b, g, t: (b, t, cb + g)),
                  pl.BlockSpec((1, q, gw), lambda b, g, t: (b, t, g)),
                  pl.BlockSpec((1, 1, 8, q), lambda b, g, t: (b, g, 0, t)),
                  pl.BlockSpec((SSD_CONV, gw), lambda b, g, t: (0, g)),
                  pl.BlockSpec((SSD_CONV, ns), lambda b, g, t: (0, cwb0 + g)),
                  pl.BlockSpec((SSD_CONV, ns), lambda b, g, t: (0, cwb0 + G + g)),
                  pl.BlockSpec((1, gw), lambda b, g, t: (0, g)),
                  pl.BlockSpec((1, ns), lambda b, g, t: (0, cwb0 + g)),
                  pl.BlockSpec((1, ns), lambda b, g, t: (0, cwb0 + G + g)),
                  pl.BlockSpec((1, 8, 1), lambda b, g, t: (g, 0, 0)),
                  pl.BlockSpec((1, 8, 1), lambda b, g, t: (g, 0, 0)),
                  pl.BlockSpec((1, gw), lambda b, g, t: (0, g)),
                  pl.BlockSpec((1, gw), lambda b, g, t: (0, g))],
        out_specs=pl.BlockSpec((1, q, gw), lambda b, g, t: (b, t, g)),
        scratch_shapes=[pltpu.VMEM((q + 8, gw + 2 * ns), F32),
                        pltpu.VMEM((ns, gw), F32)],
        compiler_params=_cparams(("parallel", "parallel", "arbitrary")),
        name="ssd",
    )(u, u, u, u, dt_rows, conv_w, conv_w, conv_w, conv_b, conv_b, conv_b,
      dt_bias, a_log, d_skip, norm_w)


def _prep_kernel(q_ref, k_ref, v_ref, qi_ref, sm_ref, kpos_ref, qnw_ref, knw_ref, aug_ref,
                 qt_ref, ka_ref, vt_ref, qih_ref, kit_ref):
    tp = q_ref.shape[1]
    dh = ATT_HEAD_DIM
    qscale = (ATT_HEAD_DIM ** -0.5) * LOG2E

    def rms(xh, w):
        return xh * lax.rsqrt(jnp.mean(xh * xh, axis=-1, keepdims=True) + EPS) * w

    qf = q_ref[0].astype(F32)
    for h in range(ATT_HEADS):
        qh = rms(qf[:, h * dh:(h + 1) * dh], qnw_ref[...]) * qscale
        lanes = slice((h % 2) * tp, (h % 2 + 1) * tp)
        qt_ref[0, h // 2, 0:dh, lanes] = jnp.transpose(qh).astype(BF16)
        qt_ref[0, h // 2, dh:QK_AUG, lanes] = jnp.broadcast_to(aug_ref[h], (QK_AUG - dh, tp)).astype(BF16)

    kp = kpos_ref[0]
    lane = lax.broadcasted_iota(jnp.int32, (tp, QK_AUG - dh), 1)
    p_hi = (kp >> 7).astype(F32)
    p_lo = (kp & 127).astype(F32)
    pos_cols = jnp.where(lane < 3, p_hi, jnp.where(lane < 6, p_lo, 0.0)).astype(BF16)
    kf = k_ref[0].astype(F32)
    vf = v_ref[0].astype(F32)
    for g in range(ATT_KV_HEADS):
        ka_ref[0, g, :, 0:dh] = rms(kf[:, g * dh:(g + 1) * dh], knw_ref[...]).astype(BF16)
        ka_ref[0, g, :, dh:QK_AUG] = pos_cols
        vt_ref[0, g, 0:dh, :] = jnp.transpose(vf[:, g * dh:(g + 1) * dh]).astype(BF16)
        vt_ref[0, g, dh:VT_ROWS, :] = jnp.ones((VT_ROWS - dh, tp), BF16)

    qif = qi_ref[0].astype(F32)
    for h in range(IDX_HEADS):
        qh = qif[:, h * IDX_HEAD_DIM:(h + 1) * IDX_HEAD_DIM]
        qih_ref[0, :, h] = qh.reshape(tp // IDX_RB, IDX_RB, IDX_HEAD_DIM).astype(BF16)
    sm_t = jnp.transpose(sm_ref[0])
    for j in range(tp // IDX_KC):
        kit_ref[0, j] = sm_t[SM_KIDX:SM_KIDX + IDX_HEAD_DIM, j * IDX_KC:(j + 1) * IDX_KC].astype(BF16)


def _alibi_aug():
    out = np.zeros((ATT_HEADS, QK_AUG - ATT_HEAD_DIM, 1), np.float32)
    for h in range(ATT_HEADS):
        s = np.float32(2.0 ** (-ALIBI_MAX_BIAS * (h + 1) / ATT_HEADS)) * np.float32(LOG2E)
        rem = np.float32(s)
        for i in range(3):
            piece = np.float32(np.asarray(rem, np.float32).astype(BF16))
            out[h, i, 0] = piece * np.float32(128.0)
            out[h, 3 + i, 0] = piece
            rem = np.float32(rem - piece)
    return jnp.asarray(out)


def _alibi_corr():
    s = [np.float32(2.0) * np.float32(2.0 ** (-ALIBI_MAX_BIAS * (h + 1) / ATT_HEADS)) * np.float32(LOG2E)
         for h in range(ATT_HEADS)]
    return jnp.asarray(np.array(s, np.float32).reshape(ATT_HEADS, 1, 1))


def _dsa_prep(u, small, kpos_col, q_norm, k_norm):
    bsz, L, _ = u.shape
    tp = min(PREP_T, L)
    nkc = L // IDX_KC
    return pl.pallas_call(
        _prep_kernel,
        out_shape=(jax.ShapeDtypeStruct((bsz, ATT_HEADS // 2, QK_AUG, 2 * L), BF16),
                   jax.ShapeDtypeStruct((bsz, ATT_KV_HEADS, L, QK_AUG), BF16),
                   jax.ShapeDtypeStruct((bsz, ATT_KV_HEADS, VT_ROWS, L), BF16),
                   jax.ShapeDtypeStruct((bsz, L // IDX_RB, IDX_HEADS, IDX_RB, IDX_HEAD_DIM), BF16),
                   jax.ShapeDtypeStruct((bsz, nkc, IDX_HEAD_DIM, IDX_KC), BF16)),
        grid=(bsz, L // tp),
        in_specs=[pl.BlockSpec((1, tp, ATT_WIDTH), lambda b, i: (b, i, OFF_Q // ATT_WIDTH)),
                  pl.BlockSpec((1, tp, ATT_KV_WIDTH), lambda b, i: (b, i, OFF_K // ATT_KV_WIDTH)),
                  pl.BlockSpec((1, tp, ATT_KV_WIDTH), lambda b, i: (b, i, OFF_V // ATT_KV_WIDTH)),
                  pl.BlockSpec((1, tp, IDX_HEADS * IDX_HEAD_DIM),
                               lambda b, i: (b, i, OFF_QI // (IDX_HEADS * IDX_HEAD_DIM))),
                  pl.BlockSpec((1, tp, LANES), lambda b, i: (b, i, 0)),
                  pl.BlockSpec((1, tp, 1), lambda b, i: (b, i, 0)),
                  pl.BlockSpec((1, ATT_HEAD_DIM), lambda b, i: (0, 0)),
                  pl.BlockSpec((1, ATT_HEAD_DIM), lambda b, i: (0, 0)),
                  pl.BlockSpec((ATT_HEADS, QK_AUG - ATT_HEAD_DIM, 1), lambda b, i: (0, 0, 0))],
        out_specs=(pl.BlockSpec((1, ATT_HEADS // 2, QK_AUG, 2 * tp), lambda b, i: (b, 0, 0, i)),
                   pl.BlockSpec((1, ATT_KV_HEADS, tp, QK_AUG), lambda b, i: (b, 0, i, 0)),
                   pl.BlockSpec((1, ATT_KV_HEADS, VT_ROWS, tp), lambda b, i: (b, 0, 0, i)),
                   pl.BlockSpec((1, tp // IDX_RB, IDX_HEADS, IDX_RB, IDX_HEAD_DIM),
                                lambda b, i: (b, i, 0, 0, 0)),
                   pl.BlockSpec((1, tp // IDX_KC, IDX_HEAD_DIM, IDX_KC), lambda b, i: (b, i, 0, 0))),
        compiler_params=_cparams(("parallel", "parallel")),
        name="dsa_prep",
    )(u, u, u, u, small, kpos_col, q_norm.reshape(1, -1), k_norm.reshape(1, -1), _alibi_aug())


_INT_MIN = -2 ** 31
_NEG_INF_KEY = -2139095041
I16 = jnp.int16
_I16_MIN, _I16_MAX, _I16_OFF = -32768, 32767, 32768


def _tree_sum(terms):
    while len(terms) > 1:
        terms = [terms[i] + terms[i + 1] for i in range(0, len(terms) - 1, 2)] + (
            [terms[-1]] if len(terms) % 2 else [])
    return terms[0]


def _indexer_kernel(qih_ref, kit_ref, sm_ref, qpos_ref, kposr_ref, o_ref,
                    hi_ref, lo_ref, lom_ref, wb_ref, bmax_ref, *, top_k):
    qi = pl.program_id(1)
    nrb = qih_ref.shape[1]
    tq = nrb * IDX_RB
    kc_w = kit_ref.shape[3]
    nkc_all = kit_ref.shape[1]
    hw = kc_w // 2
    assert hw == LANES and top_k <= kc_w and nkc_all % 2 == 0
    n_kc = jnp.minimum(((qi + 1) * tq + kc_w - 1) // kc_w, nkc_all)
    idx_scale = (IDX_HEAD_DIM * IDX_HEADS) ** -0.5

    sm = sm_ref[0]
    for h in range(IDX_HEADS):
        wb_ref[h] = jnp.broadcast_to(sm[:, SM_WIDX + h:SM_WIDX + h + 1], (tq, LANES))
    bmax_ref[...] = jnp.full(bmax_ref.shape, _INT_MIN, jnp.int32)

    def score_chunk(kc):
        kt = kit_ref[0, kc]
        kchunk = kposr_ref[0, kc] >> 6
        for rb in range(nrb):
            rows = slice(rb * IDX_RB, (rb + 1) * IDX_RB)
            s = jnp.dot(qih_ref[0, rb].reshape(IDX_HEADS * IDX_RB, IDX_HEAD_DIM), kt,
                        preferred_element_type=F32)
            t0, t1 = [], []
            for h in range(IDX_HEADS):
                rel = jnp.maximum(s[h * IDX_RB:(h + 1) * IDX_RB, :], 0.0)
                w = wb_ref[h, rows, :]
                t0.append(w * rel[:, 0:hw])
                t1.append(w * rel[:, hw:kc_w])
            score = jnp.concatenate([_tree_sum(t0), _tree_sum(t1)], axis=1) * idx_scale
            qchunk = qpos_ref[0, rows, :] >> 6
            score = jnp.where(kchunk <= qchunk, score, -jnp.inf)
            bits = pltpu.bitcast(score, jnp.int32)
            key = bits ^ ((bits >> 31) & 0x7FFFFFFF)
            hi_ref[kc, rows, :] = (key >> 16).astype(I16)
            lo_ref[kc, rows, :] = ((key & 0xFFFF) - _I16_OFF).astype(I16)
            bmax_ref[rows, :] = jnp.maximum(bmax_ref[rows, :], key)

    n_pair = (n_kc + 1) // 2

    def pair_body(pair, carry):
        score_chunk(2 * pair)
        score_chunk(2 * pair + 1)
        return carry

    lax.fori_loop(0, n_pair, pair_body, 0)

    one, zero = I16(1), I16(0)

    def count_ge(ref, cand):
        c16 = jnp.broadcast_to(jnp.minimum(cand, _I16_MAX).astype(I16), (tq, hw))

        def cnt_body(pair, acc):
            for kc in (2 * pair, 2 * pair + 1):
                k = ref[kc]
                acc = acc + jnp.where(k[:, 0:hw] >= c16, one, zero) + jnp.where(k[:, hw:kc_w] >= c16, one, zero)
            return acc

        acc = lax.fori_loop(0, n_pair, cnt_body, jnp.zeros((tq, hw), I16))
        return jnp.sum(acc.astype(jnp.int32), axis=1, keepdims=True)

    bm = bmax_ref[...]
    lbh = jnp.min(bm, axis=1, keepdims=True) >> 16
    ubh = jnp.max(bm, axis=1, keepdims=True) >> 16
    nbits = jnp.max(32 - lax.clz(ubh - lbh))

    def hi_body(i, thr):
        cand = thr + jnp.left_shift(jnp.int32(1), nbits - 1 - i)
        cnt = count_ge(hi_ref, cand)
        return jnp.where(jnp.logical_and(cnt >= top_k, cand <= ubh), cand, thr)

    h_thr = lax.fori_loop(0, nbits, hi_body, lbh)
    need = top_k - count_ge(hi_ref, h_thr + 1)

    h16 = jnp.broadcast_to(h_thr.astype(I16), (tq, kc_w))

    def lom_body(kc, carry):
        lom_ref[kc] = jnp.where(hi_ref[kc] == h16, lo_ref[kc], I16(_I16_MIN))
        return carry

    lax.fori_loop(0, 2 * n_pair, lom_body, 0)

    def lo_body(i, thr):
        cand = thr + jnp.left_shift(jnp.int32(1), 15 - i)
        return jnp.where(count_ge(lom_ref, cand) >= need, cand, thr)

    l_thr = lax.fori_loop(0, 16, lo_body, jnp.full((tq, 1), _I16_MIN, jnp.int32))
    thr = (h_thr << 16) + (l_thr + _I16_OFF)
    thr = jnp.maximum(thr, _NEG_INF_KEY + 1)
    h_sel = jnp.broadcast_to((thr >> 16).astype(I16), (tq, kc_w))
    l_sel = jnp.broadcast_to(((thr & 0xFFFF) - _I16_OFF).astype(I16), (tq, kc_w))

    def mask_body(kc, carry):
        h, l = hi_ref[kc], lo_ref[kc]
        sel = jnp.logical_or(h > h_sel, jnp.logical_and(h == h_sel, l >= l_sel))
        m = jnp.where(sel, jnp.asarray(0.0, BF16), jnp.asarray(NEG_BIG, BF16))
        r0 = pl.multiple_of(kc * kc_w, kc_w)
        o_ref[0, pl.ds(r0, kc_w), :] = jnp.transpose(m)
        return carry

    lax.fori_loop(0, n_kc, mask_body, 0)

    def fill_body(kc, carry):
        r0 = pl.multiple_of(kc * kc_w, kc_w)
        o_ref[0, pl.ds(r0, kc_w), :] = jnp.full((kc_w, tq), NEG_BIG, BF16)
        return carry

    lax.fori_loop(n_kc, nkc_all, fill_body, 0)


def _indexer(qih, kit, small, qpos_col, kpos_rows, top_k):
    bsz, L = small.shape[0], small.shape[1]
    tq = min(IDX_TQ, L)
    nkc = L // IDX_KC
    return pl.pallas_call(
        functools.partial(_indexer_kernel, top_k=top_k),
        out_shape=jax.ShapeDtypeStruct((bsz, L, L), BF16),
        grid=(bsz, L // tq),
        in_specs=[pl.BlockSpec((1, tq // IDX_RB, IDX_HEADS, IDX_RB, IDX_HEAD_DIM),
                               lambda b, i: (b, i, 0, 0, 0)),
                  pl.BlockSpec((1, nkc, IDX_HEAD_DIM, IDX_KC), lambda b, i: (b, 0, 0, 0)),
                  pl.BlockSpec((1, tq, LANES), lambda b, i: (b, i, 0)),
                  pl.BlockSpec((1, tq, 1), lambda b, i: (b, i, 0)),
                  pl.BlockSpec((1, nkc, 1, IDX_KC), lambda b, i: (b, 0, 0, 0))],
        out_specs=pl.BlockSpec((1, L, tq), lambda b, i: (b, 0, i)),
        scratch_shapes=[pltpu.VMEM((nkc, tq, IDX_KC), I16),
                        pltpu.VMEM((nkc, tq, IDX_KC), I16),
                        pltpu.VMEM((nkc, tq, IDX_KC), I16),
                        pltpu.VMEM((IDX_HEADS, tq, LANES), F32),
                        pltpu.VMEM((tq, IDX_KC), jnp.int32)],
        compiler_params=_cparams(("parallel", "arbitrary")),
        name="indexer",
    )(qih, kit, small, qpos_col, kpos_rows)


def _attn_kernel(qtab_ref, ktab_ref, qt_ref, ka_ref, vt_ref, mk_ref, kpos_ref, qpos_ref, corr_ref,
                 o_ref, bias_ref, s_ref, smax_ref, *state_refs):
    s_id = pl.program_id(1)
    qi = qtab_ref[s_id]
    ki = ktab_ref[s_id]
    tq = qt_ref.shape[3] // 2
    tk = ka_ref.shape[2]
    dh = ATT_HEAD_DIM
    n_pairs = ATT_HEADS // 2
    kmax = ((qi + 1) * tq - 1) // tk
    m_refs, acc_refs = state_refs[:n_pairs], state_refs[n_pairs:]

    @pl.when(ki == 0)
    def _():
        for j in range(n_pairs):
            m_refs[j][...] = jnp.full(m_refs[j].shape, -jnp.inf, F32)
            acc_refs[j][...] = jnp.zeros(acc_refs[j].shape, F32)

    def step(diag):
        bias = mk_ref[0].astype(F32)
        bias_ref[:, 0:tq] = bias
        bias_ref[:, tq:2 * tq] = bias
        if diag:
            later = jnp.maximum(kpos_ref[0][:, 0:1] - qpos_ref[0], 0).astype(F32)
            later = jnp.concatenate([later, later], axis=1)

        def scores(j, slot):
            s = jnp.dot(ka_ref[0, (2 * j) // ATT_RQ], qt_ref[0, j],
                        preferred_element_type=F32) + bias_ref[...]
            if diag:
                s = s - corr_ref[j] * later
            s_ref[slot] = s
            smax_ref[slot] = jnp.max(s, axis=0, keepdims=True)

        for j in range(ATT_LOOKAHEAD):
            scores(j, j)
        for j in range(n_pairs):
            if j + ATT_LOOKAHEAD < n_pairs:
                scores(j + ATT_LOOKAHEAD, (j + ATT_LOOKAHEAD) % ATT_SLOTS)
            m_old = m_refs[j][...]
            m_new = jnp.maximum(m_old, smax_ref[j % ATT_SLOTS])
            p = jnp.exp2(s_ref[j % ATT_SLOTS] - m_new).astype(BF16)
            acc_refs[j][...] = acc_refs[j][...] * jnp.exp2(m_old - m_new) + jnp.dot(
                vt_ref[0, (2 * j) // ATT_RQ], p, preferred_element_type=F32)
            m_refs[j][...] = m_new

    is_diag = (ki + 1) * tk - 1 > qi * tq

    @pl.when(is_diag)
    def _():
        step(True)

    @pl.when(jnp.logical_not(is_diag))
    def _():
        step(False)

    @pl.when(ki == kmax)
    def _():
        for h in range(ATT_HEADS):
            lanes = slice((h % 2) * tq, (h % 2 + 1) * tq)
            out = acc_refs[h // 2][0:dh, lanes] / acc_refs[h // 2][dh:dh + 1, lanes]
            o_ref[0, :, h * dh:(h + 1) * dh] = jnp.transpose(out).astype(BF16)


def _attn(qt, ka, vt, mask_t, kpos_col, qpos_row):
    bsz, L = mask_t.shape[0], mask_t.shape[1]
    tq, tk = min(ATT_TQ, L), min(ATT_TK, L)
    assert qt.shape[3] == 2 * L and min(PREP_T, L) == tq
    n_pairs = ATT_HEADS // 2
    pairs = [(i, k) for i in range(L // tq) for k in range(((i + 1) * tq - 1) // tk + 1)]
    qtab = jnp.asarray(np.array([p[0] for p in pairs], np.int32))
    ktab = jnp.asarray(np.array([p[1] for p in pairs], np.int32))
    corr = jnp.broadcast_to(_alibi_corr().reshape(n_pairs, 1, 2, 1), (n_pairs, 1, 2, tq)).reshape(
        n_pairs, 1, 2 * tq)
    grid_spec = pltpu.PrefetchScalarGridSpec(
        num_scalar_prefetch=2,
        grid=(bsz, len(pairs)),
        in_specs=[pl.BlockSpec((1, n_pairs, QK_AUG, 2 * tq), lambda b, s, qt_, kt_: (b, 0, 0, qt_[s])),
                  pl.BlockSpec((1, ATT_KV_HEADS, tk, QK_AUG), lambda b, s, qt_, kt_: (b, 0, kt_[s], 0)),
                  pl.BlockSpec((1, ATT_KV_HEADS, VT_ROWS, tk), lambda b, s, qt_, kt_: (b, 0, 0, kt_[s])),
                  pl.BlockSpec((1, tk, tq), lambda b, s, qt_, kt_: (b, kt_[s], qt_[s])),
                  pl.BlockSpec((1, tk, LANES), lambda b, s, qt_, kt_: (b, kt_[s], 0)),
                  pl.BlockSpec((1, 1, tq), lambda b, s, qt_, kt_: (b, 0, qt_[s])),
                  pl.BlockSpec((n_pairs, 1, 2 * tq), lambda b, s, qt_, kt_: (0, 0, 0))],
        out_specs=pl.BlockSpec((1, tq, ATT_WIDTH), lambda b, s, qt_, kt_: (b, qt_[s], 0)),
        scratch_shapes=([pltpu.VMEM((tk, 2 * tq), F32), pltpu.VMEM((ATT_SLOTS, tk, 2 * tq), F32),
                         pltpu.VMEM((ATT_SLOTS, 1, 2 * tq), F32)]
                        + [pltpu.VMEM((1, 2 * tq), F32) for _ in range(n_pairs)]
                        + [pltpu.VMEM((VT_ROWS, 2 * tq), F32) for _ in range(n_pairs)]))
    return pl.pallas_call(
        _attn_kernel,
        out_shape=jax.ShapeDtypeStruct((bsz, L, ATT_WIDTH), BF16),
        grid_spec=grid_spec,
        compiler_params=_cparams(("parallel", "arbitrary")),
        name="attn",
    )(qtab, ktab, qt, ka, vt, mask_t, kpos_col, qpos_row, corr)


def _merge_kernel(ys_ref, ya_ref, g0_ref, g1_ref, ws_ref, wa_ref, o_ref):
    ps = jnp.dot(ys_ref[0], ws_ref[...], preferred_element_type=F32)
    pa = jnp.dot(ya_ref[0], wa_ref[...], preferred_element_type=F32)
    g0 = _sigmoid(g0_ref[0].astype(F32))
    g1 = _sigmoid(g1_ref[0].astype(F32))
    o_ref[0] = (g0 * ps + g1 * pa).astype(BF16)


def _merge(y_ssd, y_att, u, w_s, w_a):
    bsz, L, _ = y_ssd.shape
    d = w_s.shape[1]
    tm, tn = min(MRG_TM, L), min(MRG_TN, d)
    g0b = OFF_GATE // tn
    g1b = (OFF_GATE + d) // tn
    return pl.pallas_call(
        _merge_kernel,
        out_shape=jax.ShapeDtypeStruct((bsz, L, d), BF16),
        grid=(bsz, L // tm, d // tn),
        in_specs=[pl.BlockSpec((1, tm, SSD_D_INNER), lambda b, i, j: (b, i, 0)),
                  pl.BlockSpec((1, tm, ATT_WIDTH), lambda b, i, j: (b, i, 0)),
                  pl.BlockSpec((1, tm, tn), lambda b, i, j: (b, i, g0b + j)),
                  pl.BlockSpec((1, tm, tn), lambda b, i, j: (b, i, g1b + j)),
                  pl.BlockSpec((SSD_D_INNER, tn), lambda b, i, j: (0, j)),
                  pl.BlockSpec((ATT_WIDTH, tn), lambda b, i, j: (0, j))],
        out_specs=pl.BlockSpec((1, tm, tn), lambda b, i, j: (b, i, j)),
        compiler_params=_cparams(("parallel", "parallel", "arbitrary")),
        name="merge",
    )(y_ssd, y_att, u, u, w_s, w_a)


def _outproj_kernel(m_ref, h_ref, g_ref, w_ref, o_ref):
    o_ref[0] = h_ref[0] + g_ref[0] * jnp.dot(m_ref[0], w_ref[...], preferred_element_type=F32)


def _outproj(merged, h, gate, w_out):
    bsz, L, d = h.shape
    tm, tn = min(MRG_TM, L), min(MRG_TN, d)
    return pl.pallas_call(
        _outproj_kernel,
        out_shape=jax.ShapeDtypeStruct((bsz, L, d), F32),
        grid=(bsz, L // tm, d // tn),
        in_specs=[pl.BlockSpec((1, tm, d), lambda b, i, j: (b, i, 0)),
                  pl.BlockSpec((1, tm, tn), lambda b, i, j: (b, i, j)),
                  pl.BlockSpec((1, 1, tn), lambda b, i, j: (b, 0, j)),
                  pl.BlockSpec((d, tn), lambda b, i, j: (0, j))],
        out_specs=pl.BlockSpec((1, tm, tn), lambda b, i, j: (b, i, j)),
        compiler_params=_cparams(("parallel", "parallel", "arbitrary")),
        name="outproj",
    )(merged, h, gate, w_out)


def _split_w_in(w_in, d):
    sizes = (SSD_D_INNER, SSD_D_INNER + 2 * SSD_GROUPS * SSD_STATE, SSD_HEADS, ATT_WIDTH, ATT_KV_WIDTH,
             ATT_KV_WIDTH, IDX_HEADS * IDX_HEAD_DIM, IDX_HEAD_DIM, IDX_HEADS, 2 * d)
    offs = np.concatenate([[0], np.cumsum(sizes)])
    seg = [w_in[:, int(offs[i]):int(offs[i + 1])] for i in range(len(sizes))]
    w_z, w_xbc, w_dt, w_q, w_k, w_v, w_qi, w_ki, w_wi, w_g = seg
    w_main = jnp.concatenate([w_z, w_xbc, w_q, w_k, w_v, w_qi, w_g], axis=1).astype(BF16)
    pad = jnp.zeros((w_in.shape[0], LANES - (IDX_HEAD_DIM + IDX_HEADS + SSD_HEADS)), w_in.dtype)
    w_small = jnp.concatenate([w_ki, w_wi, w_dt, pad], axis=1).astype(BF16)
    return w_main, w_small


def kernel(x, c, positions, w_ada, b_ada, norm_ffn1, ffn1_w1, ffn1_w3, ffn1_w2, norm_mix, w_in,
           ssd_conv_w, ssd_conv_b, ssd_dt_bias, ssd_a_log, ssd_d, ssd_norm, q_norm, k_norm,
           w_br_ssd, w_br_att, w_out, norm_ffn2, ffn2_w1, ffn2_w3, ffn2_w2):
    bsz, L, d = x.shape
    depth = w_ada.shape[0]
    top_k = min(TOPK_MAX, L // 4)
    G, R = SSD_GROUPS, SSD_R
    kpos_col = positions.reshape(bsz, L, 1)
    kpos_lanes = jnp.broadcast_to(kpos_col, (bsz, L, LANES))
    qpos_row = positions.reshape(bsz, 1, L)
    kpos_rows = positions.reshape(bsz, L // IDX_KC, 1, IDX_KC)

    def pad_rows(a):
        return jnp.pad(a.reshape(G, R), ((0, 0), (0, 8 - R))).reshape(G, 8, 1)

    h = x
    for l in range(depth):
        mod = _ada(c, w_ada[l], b_ada[l]).reshape(bsz, N_MOD, 1, d)
        sh1, sc1, g1, sh2, sc2, g2, sh3, sc3, g3 = [mod[:, i] for i in range(N_MOD)]

        h = _ffn(h, norm_ffn1[l].reshape(1, d), sh1, sc1, g1,
                 ffn1_w1[l].astype(BF16), ffn1_w3[l].astype(BF16), ffn1_w2[l].astype(BF16))

        w_main, w_small = _split_w_in(w_in[l], d)
        u, small = _inproj(h, norm_mix[l].reshape(1, d), sh2, sc2, w_main, w_small)

        dt_raw = small[:, :, SM_DT:SM_DT + SSD_HEADS].reshape(bsz, L, G, R)
        dt_rows = jnp.pad(jnp.transpose(dt_raw, (0, 2, 3, 1)), ((0, 0), (0, 0), (0, 8 - R), (0, 0)))
        y_ssd = _ssd(u, dt_rows, ssd_conv_w[l], ssd_conv_b[l].reshape(1, -1),
                     pad_rows(ssd_dt_bias[l]), pad_rows(ssd_a_log[l]),
                     jnp.repeat(ssd_d[l], SSD_HEAD_DIM).reshape(1, SSD_D_INNER),
                     ssd_norm[l].reshape(1, SSD_D_INNER))

        qt, ka, vt, qih, kit = _dsa_prep(u, small, kpos_col, q_norm[l], k_norm[l])
        mask_t = _indexer(qih, kit, small, kpos_col, kpos_rows, top_k)
        y_att = _attn(qt, ka, vt, mask_t, kpos_lanes, qpos_row)

        merged = _merge(y_ssd, y_att, u, w_br_ssd[l].astype(BF16), w_br_att[l].astype(BF16))
        h = _outproj(merged, h, g2, w_out[l].astype(BF16))

        h = _ffn(h, norm_ffn2[l].reshape(1, d), sh3, sc3, g3,
                 ffn2_w1[l].astype(BF16), ffn2_w3[l].astype(BF16), ffn2_w2[l].astype(BF16))
    return h
```

```python
import functools
import math

import numpy as np
import jax
import jax.numpy as jnp
from jax import lax
from jax.experimental import pallas as pl
from jax.experimental.pallas import tpu as pltpu

F32 = jnp.float32
BF16 = jnp.bfloat16

CHUNK = 64
EPS = 1e-6
N_MOD = 9
SSD_D_INNER = 2048
SSD_HEAD_DIM = 64
SSD_GROUPS = 8
SSD_HEADS = SSD_D_INNER // SSD_HEAD_DIM
SSD_R = SSD_HEADS // SSD_GROUPS
SSD_GW = SSD_D_INNER // SSD_GROUPS
SSD_STATE = 128
SSD_CONV = 4
ATT_HEADS = 16
ATT_KV_HEADS = 4
ATT_RQ = ATT_HEADS // ATT_KV_HEADS
ATT_HEAD_DIM = 128
ATT_WIDTH = ATT_HEADS * ATT_HEAD_DIM
ATT_KV_WIDTH = ATT_KV_HEADS * ATT_HEAD_DIM
IDX_HEADS = 16
IDX_HEAD_DIM = 64
TOPK_MAX = 256
ALIBI_MAX_BIAS = 8.0
LOG2E = 1.4426950408889634
NEG_BIG = -1e30

LANES = 128
QK_AUG = 256
VT_ROWS = ATT_HEAD_DIM + 16
VMEM_LIMIT = 56 * 1024 * 1024

OFF_Z = 0
OFF_XBC = OFF_Z + SSD_D_INNER
OFF_Q = OFF_XBC + SSD_D_INNER + 2 * SSD_GROUPS * SSD_STATE
OFF_K = OFF_Q + ATT_WIDTH
OFF_V = OFF_K + ATT_KV_WIDTH
OFF_QI = OFF_V + ATT_KV_WIDTH
OFF_GATE = OFF_QI + IDX_HEADS * IDX_HEAD_DIM
SM_KIDX = 0
SM_WIDX = IDX_HEAD_DIM
SM_DT = SM_WIDX + IDX_HEADS

FFN_TM, FFN_TF = 512, 512
NORM_RB = 128
INP_TM, INP_TN = 1024, 1024
SSD_Q = 256
PREP_T = 256
IDX_TQ, IDX_KC, IDX_RB = 128, 256, 16
ATT_TQ, ATT_TK = 256, 512
ATT_LOOKAHEAD = 2
ATT_SLOTS = ATT_LOOKAHEAD + 1
MRG_TM, MRG_TN = 1024, 512


def _cparams(sem):
    return pltpu.CompilerParams(dimension_semantics=sem, vmem_limit_bytes=VMEM_LIMIT)


def _sigmoid(x):
    return 1.0 / (1.0 + jnp.exp(-x))


def _rms_mod(x, nw, sh, sc):
    ms = jnp.mean(x * x, axis=-1, keepdims=True)
    return (x * lax.rsqrt(ms + EPS) * nw) * (1.0 + sc) + sh


def _rms_mod_rows(x_ref, nw_ref, sh_ref, sc_ref, hn_ref):
    tm = hn_ref.shape[0]
    rb = min(NORM_RB, tm)

    def body(i, carry):
        rows = pl.ds(pl.multiple_of(i * rb, rb), rb)
        hn_ref[rows, :] = _rms_mod(x_ref[0, rows, :], nw_ref[...], sh_ref[0], sc_ref[0]).astype(BF16)
        return carry

    lax.fori_loop(0, tm // rb, body, 0)


def _ada_kernel(ct_ref, w_ref, b_ref, o_ref):
    ct = ct_ref[...]
    ca = ct * _sigmoid(ct)
    w = w_ref[...]
    rows = [jnp.sum(w * ca[:, b:b + 1], axis=0, keepdims=True) for b in range(ct.shape[1])]
    o_ref[...] = jnp.concatenate(rows, axis=0) + b_ref[...]


def _ada(c, w_ada, b_ada):
    bsz, d = c.shape
    n = w_ada.shape[1]
    tn = 1024 if n % 1024 == 0 else n
    return pl.pallas_call(
        _ada_kernel,
        out_shape=jax.ShapeDtypeStruct((bsz, n), F32),
        grid=(n // tn,),
        in_specs=[pl.BlockSpec((d, bsz), lambda j: (0, 0)),
                  pl.BlockSpec((d, tn), lambda j: (0, j)),
                  pl.BlockSpec((1, tn), lambda j: (0, j))],
        out_specs=pl.BlockSpec((bsz, tn), lambda j: (0, j)),
        compiler_params=_cparams(("arbitrary",)),
        name="ada",
    )(c.T, w_ada, b_ada.reshape(1, n))


def _ffn_kernel(x_ref, nw_ref, sh_ref, sc_ref, g_ref, w1_ref, w3_ref, w2_ref, o_ref, hn_ref):
    f = pl.program_id(2)

    @pl.when(f == 0)
    def _():
        _rms_mod_rows(x_ref, nw_ref, sh_ref, sc_ref, hn_ref)
        o_ref[0] = jnp.zeros(o_ref.shape[1:], F32)

    hn = hn_ref[...]
    a = jnp.dot(hn, w1_ref[...], preferred_element_type=F32)
    b = jnp.dot(hn, w3_ref[...], preferred_element_type=F32)
    g = (a * _sigmoid(a) * b).astype(BF16)
    o_ref[0] += jnp.dot(g, w2_ref[...], preferred_element_type=F32)

    @pl.when(f == pl.num_programs(2) - 1)
    def _():
        o_ref[0] = x_ref[0] + 0.5 * g_ref[0] * o_ref[0]


def _ffn(h, nw, sh, sc, gate, w1, w3, w2):
    bsz, L, d = h.shape
    ff = w1.shape[1]
    tm, tf = min(FFN_TM, L), min(FFN_TF, ff)
    vec = pl.BlockSpec((1, 1, d), lambda b, i, f: (b, 0, 0))
    return pl.pallas_call(
        _ffn_kernel,
        out_shape=jax.ShapeDtypeStruct((bsz, L, d), F32),
        grid=(bsz, L // tm, ff // tf),
        in_specs=[pl.BlockSpec((1, tm, d), lambda b, i, f: (b, i, 0)),
                  pl.BlockSpec((1, d), lambda b, i, f: (0, 0)),
                  vec, vec, vec,
                  pl.BlockSpec((d, tf), lambda b, i, f: (0, f)),
                  pl.BlockSpec((d, tf), lambda b, i, f: (0, f)),
                  pl.BlockSpec((tf, d), lambda b, i, f: (f, 0))],
        out_specs=pl.BlockSpec((1, tm, d), lambda b, i, f: (b, i, 0)),
        scratch_shapes=[pltpu.VMEM((tm, d), BF16)],
        compiler_params=_cparams(("parallel", "parallel", "arbitrary")),
        name="ffn",
    )(h, nw, sh, sc, gate, w1, w3, w2)


def _inproj_kernel(x_ref, nw_ref, sh_ref, sc_ref, w_ref, ws_ref, u_ref, s_ref, hn_ref):
    j = pl.program_id(2)

    @pl.when(j == 0)
    def _():
        _rms_mod_rows(x_ref, nw_ref, sh_ref, sc_ref, hn_ref)
        s_ref[0] = jnp.dot(hn_ref[...], ws_ref[...], preferred_element_type=F32)

    u_ref[0] = jnp.dot(hn_ref[...], w_ref[...], preferred_element_type=F32).astype(BF16)


def _inproj(h, nw, sh, sc, w_main, w_small):
    bsz, L, d = h.shape
    n = w_main.shape[1]
    tm = min(INP_TM, L)
    tn = INP_TN if n % INP_TN == 0 else INP_TN // 2
    vec = pl.BlockSpec((1, 1, d), lambda b, i, j: (b, 0, 0))
    return pl.pallas_call(
        _inproj_kernel,
        out_shape=(jax.ShapeDtypeStruct((bsz, L, n), BF16),
                   jax.ShapeDtypeStruct((bsz, L, LANES), F32)),
        grid=(bsz, L // tm, n // tn),
        in_specs=[pl.BlockSpec((1, tm, d), lambda b, i, j: (b, i, 0)),
                  pl.BlockSpec((1, d), lambda b, i, j: (0, 0)),
                  vec, vec,
                  pl.BlockSpec((d, tn), lambda b, i, j: (0, j)),
                  pl.BlockSpec((d, LANES), lambda b, i, j: (0, 0))],
        out_specs=(pl.BlockSpec((1, tm, tn), lambda b, i, j: (b, i, j)),
                   pl.BlockSpec((1, tm, LANES), lambda b, i, j: (b, i, 0))),
        scratch_shapes=[pltpu.VMEM((tm, d), BF16)],
        compiler_params=_cparams(("parallel", "parallel", "arbitrary")),
        name="inproj",
    )(h, nw, sh, sc, w_main, w_small)


def _ssd_kernel(xs_ref, bm_ref, cm_ref, z_ref, dtr_ref, cwx_ref, cwb_ref, cwc_ref,
                cbx_ref, cbb_ref, cbc_ref, dtb_ref, alog_ref, dsk_ref, nw_ref,
                o_ref, ext_ref, state_ref):
    t = pl.program_id(2)
    q = xs_ref.shape[1]
    gw, ns = SSD_GW, SSD_STATE
    cw = gw + 2 * ns

    @pl.when(t == 0)
    def _():
        ext_ref[0:8, :] = jnp.zeros((8, cw), F32)
        state_ref[...] = jnp.zeros(state_ref.shape, F32)

    ext_ref[8:8 + q, 0:gw] = xs_ref[0].astype(F32)
    ext_ref[8:8 + q, gw:gw + ns] = bm_ref[0].astype(F32)
    ext_ref[8:8 + q, gw + ns:cw] = cm_ref[0].astype(F32)
    wts = jnp.concatenate([cwx_ref[...], cwb_ref[...], cwc_ref[...]], axis=1)
    acc = jnp.concatenate([cbx_ref[...], cbb_ref[...], cbc_ref[...]], axis=1)
    for j in range(SSD_CONV):
        acc = acc + wts[j:j + 1, :] * ext_ref[8 - (SSD_CONV - 1) + j:8 - (SSD_CONV - 1) + j + q, :]
    tail = ext_ref[q:q + 8, :]
    ext_ref[0:8, :] = tail
    xc = acc * _sigmoid(acc)
    xs = xc[:, 0:gw]
    bm = xc[:, gw:gw + ns].astype(BF16)
    cm = xc[:, gw + ns:cw].astype(BF16)
    xs_b = xs.astype(BF16)

    dtx = dtr_ref[0, 0] + dtb_ref[0]
    dt_row = jnp.maximum(dtx, 0.0) + jnp.log(1.0 + jnp.exp(-jnp.abs(dtx)))
    dta_row = dt_row * (-jnp.exp(alog_ref[0]))

    ti = lax.broadcasted_iota(jnp.int32, (q, q), 0)
    si = lax.broadcasted_iota(jnp.int32, (q, q), 1)
    tril = si <= ti
    eye = si == ti
    lane_head = lax.broadcasted_iota(jnp.int32, (1, gw), 1) // SSD_HEAD_DIM

    cb = lax.dot_general(cm, bm, (((1,), (1,)), ((), ())), preferred_element_type=F32)

    y = jnp.zeros((q, gw), F32)
    f_exp = jnp.zeros((q, gw), F32)
    f_w = jnp.zeros((q, gw), F32)
    e_dec = jnp.zeros((1, gw), F32)
    for r in range(SSD_R):
        dta_r = dta_row[r:r + 1, :]
        dt_r = dt_row[r:r + 1, :]
        acum_c = jnp.sum(jnp.where(tril, dta_r, 0.0), axis=1, keepdims=True)
        acum_r = jnp.sum(jnp.where(eye, acum_c, 0.0), axis=0, keepdims=True)
        dt_c = jnp.sum(jnp.where(eye, dt_r, 0.0), axis=1, keepdims=True)
        decay = jnp.exp(jnp.where(tril, acum_c - acum_r, -jnp.inf))
        m_r = (cb * decay * dt_r).astype(BF16)
        yd = jnp.dot(m_r, xs_b, preferred_element_type=F32)
        sel = lane_head == r
        y = y + jnp.where(sel, yd, 0.0)
        a_last = acum_r[:, q - 1:q]
        f_exp = f_exp + jnp.where(sel, jnp.exp(acum_c), 0.0)
        f_w = f_w + jnp.where(sel, dt_c * jnp.exp(a_last - acum_c), 0.0)
        e_dec = e_dec + jnp.where(sel, jnp.exp(a_last), 0.0)

    state = state_ref[...]
    y = y + jnp.dot(cm, state.astype(BF16), preferred_element_type=F32) * f_exp
    xw = (xs * f_w).astype(BF16)
    bm_t = jnp.transpose(xc[:, gw:gw + ns]).astype(BF16)
    state_ref[...] = state * e_dec + jnp.dot(bm_t, xw, preferred_element_type=F32)

    y = y + dsk_ref[...] * xs
    zf = z_ref[0].astype(F32)
    y = y * (zf * _sigmoid(zf))
    ms = jnp.mean(y * y, axis=-1, keepdims=True)
    o_ref[0] = (y * lax.rsqrt(ms + EPS) * nw_ref[...]).astype(BF16)


def _ssd(u, dt_rows, conv_w, conv_b, dt_bias, a_log, d_skip, norm_w):
    bsz, L, _ = u.shape
    q = min(SSD_Q, L)
    gw, ns, G = SSD_GW, SSD_STATE, SSD_GROUPS
    xb = OFF_XBC // gw
    bb = (OFF_XBC + SSD_D_INNER) // ns
    cb = bb + G
    cwb0 = SSD_D_INNER // ns
    return pl.pallas_call(
        _ssd_kernel,
        out_shape=jax.ShapeDtypeStruct((bsz, L, SSD_D_INNER), BF16),
        grid=(bsz, G, L // q),
        in_specs=[pl.BlockSpec((1, q, gw), lambda b, g, t: (b, t, xb + g)),
                  pl.BlockSpec((1, q, ns), lambda b, g, t: (b, t, bb + g)),
                  pl.BlockSpec((1, q, ns), lambda b, g, t: (b, t, cb + g)),
                  pl.BlockSpec((1, q, gw), lambda b, g, t: (b, t, g)),
                  pl.BlockSpec((1, 1, 8, q), lambda b, g, t: (b, g, 0, t)),
                  pl.BlockSpec((SSD_CONV, gw), lambda b, g, t: (0, g)),
                  pl.BlockSpec((SSD_CONV, ns), lambda b, g, t: (0, cwb0 + g)),
                  pl.BlockSpec((SSD_CONV, ns), lambda b, g, t: (0, cwb0 + G + g)),
                  pl.BlockSpec((1, gw), lambda b, g, t: (0, g)),
                  pl.BlockSpec((1, ns), lambda b, g, t: (0, cwb0 + g)),
                  pl.BlockSpec((1, ns), lambda b, g, t: (0, cwb0 + G + g)),
                  pl.BlockSpec((1, 8, 1), lambda b, g, t: (g, 0, 0)),
                  pl.BlockSpec((1, 8, 1), lambda b, g, t: (g, 0, 0)),
                  pl.BlockSpec((1, gw), lambda b, g, t: (0, g)),
                  pl.BlockSpec((1, gw), lambda b, g, t: (0, g))],
        out_specs=pl.BlockSpec((1, q, gw), lambda b, g, t: (b, t, g)),
        scratch_shapes=[pltpu.VMEM((q + 8, gw + 2 * ns), F32),
                        pltpu.VMEM((ns, gw), F32)],
        compiler_params=_cparams(("parallel", "parallel", "arbitrary")),
        name="ssd",
    )(u, u, u, u, dt_rows, conv_w, conv_w, conv_w, conv_b, conv_b, conv_b,
      dt_bias, a_log, d_skip, norm_w)


def _prep_kernel(q_ref, k_ref, v_ref, qi_ref, sm_ref, kpos_ref, qnw_ref, knw_ref, aug_ref,
                 qt_ref, ka_ref, vt_ref, qih_ref, kit_ref):
    tp = q_ref.shape[1]
    dh = ATT_HEAD_DIM
    qscale = (ATT_HEAD_DIM ** -0.5) * LOG2E

    def rms(xh, w):
        return xh * lax.rsqrt(jnp.mean(xh * xh, axis=-1, keepdims=True) + EPS) * w

    qf = q_ref[0].astype(F32)
    for h in range(ATT_HEADS):
        qh = rms(qf[:, h * dh:(h + 1) * dh], qnw_ref[...]) * qscale
        lanes = slice((h % 2) * tp, (h % 2 + 1) * tp)
        qt_ref[0, h // 2, 0:dh, lanes] = jnp.transpose(qh).astype(BF16)
        qt_ref[0, h // 2, dh:QK_AUG, lanes] = jnp.broadcast_to(aug_ref[h], (QK_AUG - dh, tp)).astype(BF16)

    kp = kpos_ref[0]
    lane = lax.broadcasted_iota(jnp.int32, (tp, QK_AUG - dh), 1)
    p_hi = (kp >> 7).astype(F32)
    p_lo = (kp & 127).astype(F32)
    pos_cols = jnp.where(lane < 3, p_hi, jnp.where(lane < 6, p_lo, 0.0)).astype(BF16)
    kf = k_ref[0].astype(F32)
    vf = v_ref[0].astype(F32)
    for g in range(ATT_KV_HEADS):
        ka_ref[0, g, :, 0:dh] = rms(kf[:, g * dh:(g + 1) * dh], knw_ref[...]).astype(BF16)
        ka_ref[0, g, :, dh:QK_AUG] = pos_cols
        vt_ref[0, g, 0:dh, :] = jnp.transpose(vf[:, g * dh:(g + 1) * dh]).astype(BF16)
        vt_ref[0, g, dh:VT_ROWS, :] = jnp.ones((VT_ROWS - dh, tp), BF16)

    qif = qi_ref[0].astype(F32)
    for h in range(IDX_HEADS):
        qh = qif[:, h * IDX_HEAD_DIM:(h + 1) * IDX_HEAD_DIM]
        qih_ref[0, :, h] = qh.reshape(tp // IDX_RB, IDX_RB, IDX_HEAD_DIM).astype(BF16)
    sm_t = jnp.transpose(sm_ref[0])
    for j in range(tp // IDX_KC):
        kit_ref[0, j] = sm_t[SM_KIDX:SM_KIDX + IDX_HEAD_DIM, j * IDX_KC:(j + 1) * IDX_KC].astype(BF16)


def _alibi_aug():
    out = np.zeros((ATT_HEADS, QK_AUG - ATT_HEAD_DIM, 1), np.float32)
    for h in range(ATT_HEADS):
        s = np.float32(2.0 ** (-ALIBI_MAX_BIAS * (h + 1) / ATT_HEADS)) * np.float32(LOG2E)
        rem = np.float32(s)
        for i in range(3):
            piece = np.float32(np.asarray(rem, np.float32).astype(BF16))
            out[h, i, 0] = piece * np.float32(128.0)
            out[h, 3 + i, 0] = piece
            rem = np.float32(rem - piece)
    return jnp.asarray(out)


def _alibi_corr():
    s = [np.float32(2.0) * np.float32(2.0 ** (-ALIBI_MAX_BIAS * (h + 1) / ATT_HEADS)) * np.float32(LOG2E)
         for h in range(ATT_HEADS)]
    return jnp.asarray(np.array(s, np.float32).reshape(ATT_HEADS, 1, 1))


def _dsa_prep(u, small, kpos_col, q_norm, k_norm):
    bsz, L, _ = u.shape
    tp = min(PREP_T, L)
    nkc = L // IDX_KC
    return pl.pallas_call(
        _prep_kernel,
        out_shape=(jax.ShapeDtypeStruct((bsz, ATT_HEADS // 2, QK_AUG, 2 * L), BF16),
                   jax.ShapeDtypeStruct((bsz, ATT_KV_HEADS, L, QK_AUG), BF16),
                   jax.ShapeDtypeStruct((bsz, ATT_KV_HEADS, VT_ROWS, L), BF16),
                   jax.ShapeDtypeStruct((bsz, L // IDX_RB, IDX_HEADS, IDX_RB, IDX_HEAD_DIM), BF16),
                   jax.ShapeDtypeStruct((bsz, nkc, IDX_HEAD_DIM, IDX_KC), BF16)),
        grid=(bsz, L // tp),
        in_specs=[pl.BlockSpec((1, tp, ATT_WIDTH), lambda b, i: (b, i, OFF_Q // ATT_WIDTH)),
                  pl.BlockSpec((1, tp, ATT_KV_WIDTH), lambda b, i: (b, i, OFF_K // ATT_KV_WIDTH)),
                  pl.BlockSpec((1, tp, ATT_KV_WIDTH), lambda b, i: (b, i, OFF_V // ATT_KV_WIDTH)),
                  pl.BlockSpec((1, tp, IDX_HEADS * IDX_HEAD_DIM),
                               lambda b, i: (b, i, OFF_QI // (IDX_HEADS * IDX_HEAD_DIM))),
                  pl.BlockSpec((1, tp, LANES), lambda b, i: (b, i, 0)),
                  pl.BlockSpec((1, tp, 1), lambda b, i: (b, i, 0)),
                  pl.BlockSpec((1, ATT_HEAD_DIM), lambda b, i: (0, 0)),
                  pl.BlockSpec((1, ATT_HEAD_DIM), lambda b, i: (0, 0)),
                  pl.BlockSpec((ATT_HEADS, QK_AUG - ATT_HEAD_DIM, 1), lambda b, i: (0, 0, 0))],
        out_specs=(pl.BlockSpec((1, ATT_HEADS // 2, QK_AUG, 2 * tp), lambda b, i: (b, 0, 0, i)),
                   pl.BlockSpec((1, ATT_KV_HEADS, tp, QK_AUG), lambda b, i: (b, 0, i, 0)),
                   pl.BlockSpec((1, ATT_KV_HEADS, VT_ROWS, tp), lambda b, i: (b, 0, 0, i)),
                   pl.BlockSpec((1, tp // IDX_RB, IDX_HEADS, IDX_RB, IDX_HEAD_DIM),
                                lambda b, i: (b, i, 0, 0, 0)),
                   pl.BlockSpec((1, tp // IDX_KC, IDX_HEAD_DIM, IDX_KC), lambda b, i: (b, i, 0, 0))),
        compiler_params=_cparams(("parallel", "parallel")),
        name="dsa_prep",
    )(u, u, u, u, small, kpos_col, q_norm.reshape(1, -1), k_norm.reshape(1, -1), _alibi_aug())


_INT_MIN = -2 ** 31
_NEG_INF_KEY = -2139095041


def _tree_sum(terms):
    while len(terms) > 1:
        terms = [terms[i] + terms[i + 1] for i in range(0, len(terms) - 1, 2)] + (
            [terms[-1]] if len(terms) % 2 else [])
    return terms[0]


def _indexer_kernel(qih_ref, kit_ref, sm_ref, qpos_ref, kposr_ref, o_ref,
                    keys_ref, wb_ref, bmax_ref, *, top_k):
    qi = pl.program_id(1)
    nrb = qih_ref.shape[1]
    tq = nrb * IDX_RB
    kc_w = kit_ref.shape[3]
    nkc_all = kit_ref.shape[1]
    hw = kc_w // 2
    assert hw == LANES and top_k <= kc_w and nkc_all % 2 == 0
    n_kc = jnp.minimum(((qi + 1) * tq + kc_w - 1) // kc_w, nkc_all)
    idx_scale = (IDX_HEAD_DIM * IDX_HEADS) ** -0.5

    sm = sm_ref[0]
    for h in range(IDX_HEADS):
        wb_ref[h] = jnp.broadcast_to(sm[:, SM_WIDX + h:SM_WIDX + h + 1], (tq, LANES))
    bmax_ref[...] = jnp.full(bmax_ref.shape, _INT_MIN, jnp.int32)

    def score_chunk(kc):
        kt = kit_ref[0, kc]
        kchunk = kposr_ref[0, kc] >> 6
        for rb in range(nrb):
            rows = slice(rb * IDX_RB, (rb + 1) * IDX_RB)
            s = jnp.dot(qih_ref[0, rb].reshape(IDX_HEADS * IDX_RB, IDX_HEAD_DIM), kt,
                        preferred_element_type=F32)
            t0, t1 = [], []
            for h in range(IDX_HEADS):
                rel = jnp.maximum(s[h * IDX_RB:(h + 1) * IDX_RB, :], 0.0)
                w = wb_ref[h, rows, :]
                t0.append(w * rel[:, 0:hw])
                t1.append(w * rel[:, hw:kc_w])
            score = jnp.concatenate([_tree_sum(t0), _tree_sum(t1)], axis=1) * idx_scale
            qchunk = qpos_ref[0, rows, :] >> 6
            score = jnp.where(kchunk <= qchunk, score, -jnp.inf)
            bits = pltpu.bitcast(score, jnp.int32)
            key = bits ^ ((bits >> 31) & 0x7FFFFFFF)
            keys_ref[kc, rows, :] = key
            bmax_ref[rows, :] = jnp.maximum(bmax_ref[rows, :], key)

    n_pair = (n_kc + 1) // 2

    def pair_body(pair, carry):
        score_chunk(2 * pair)
        score_chunk(2 * pair + 1)
        return carry

    lax.fori_loop(0, n_pair, pair_body, 0)

    bm = bmax_ref[...]
    lb = jnp.min(bm, axis=1, keepdims=True)
    ub = jnp.max(bm, axis=1, keepdims=True)
    span = ub - lb
    wrapped = span < 0
    lb = jnp.where(wrapped, _INT_MIN, lb)
    nb = jnp.where(wrapped, 32, 32 - lax.clz(span))
    nbits = jnp.max(nb)

    def bit_body(i, thr):
        cand = thr + jnp.left_shift(jnp.int32(1), nbits - 1 - i)
        candb = jnp.broadcast_to(cand, (tq, hw))

        def cnt_body(pair, acc):
            for kc in (2 * pair, 2 * pair + 1):
                k = keys_ref[kc]
                acc = acc + jnp.where(k[:, 0:hw] >= candb, 1, 0) + jnp.where(k[:, hw:kc_w] >= candb, 1, 0)
            return acc

        acc = lax.fori_loop(0, n_pair, cnt_body, jnp.zeros((tq, hw), jnp.int32))
        cnt = jnp.sum(acc, axis=1, keepdims=True)
        return jnp.where(jnp.logical_and(cnt >= top_k, cand > thr), cand, thr)

    thr = lax.fori_loop(0, nbits, bit_body, lb)
    thr = jnp.maximum(thr, _NEG_INF_KEY + 1)
    thr_b = jnp.broadcast_to(thr, (tq, hw))
    thr_full = jnp.concatenate([thr_b] * (kc_w // hw), axis=1)

    def mask_body(kc, carry):
        m = jnp.where(keys_ref[kc] >= thr_full, 0.0, NEG_BIG).astype(BF16)
        r0 = pl.multiple_of(kc * kc_w, kc_w)
        o_ref[0, pl.ds(r0, kc_w), :] = jnp.transpose(m)
        return carry

    lax.fori_loop(0, n_kc, mask_body, 0)

    def fill_body(kc, carry):
        r0 = pl.multiple_of(kc * kc_w, kc_w)
        o_ref[0, pl.ds(r0, kc_w), :] = jnp.full((kc_w, tq), NEG_BIG, BF16)
        return carry

    lax.fori_loop(n_kc, nkc_all, fill_body, 0)


def _indexer(qih, kit, small, qpos_col, kpos_rows, top_k):
    bsz, L = small.shape[0], small.shape[1]
    tq = min(IDX_TQ, L)
    nkc = L // IDX_KC
    return pl.pallas_call(
        functools.partial(_indexer_kernel, top_k=top_k),
        out_shape=jax.ShapeDtypeStruct((bsz, L, L), BF16),
        grid=(bsz, L // tq),
        in_specs=[pl.BlockSpec((1, tq // IDX_RB, IDX_HEADS, IDX_RB, IDX_HEAD_DIM),
                               lambda b, i: (b, i, 0, 0, 0)),
                  pl.BlockSpec((1, nkc, IDX_HEAD_DIM, IDX_KC), lambda b, i: (b, 0, 0, 0)),
                  pl.BlockSpec((1, tq, LANES), lambda b, i: (b, i, 0)),
                  pl.BlockSpec((1, tq, 1), lambda b, i: (b, i, 0)),
                  pl.BlockSpec((1, nkc, 1, IDX_KC), lambda b, i: (b, 0, 0, 0))],
        out_specs=pl.BlockSpec((1, L, tq), lambda b, i: (b, 0, i)),
        scratch_shapes=[pltpu.VMEM((nkc, tq, IDX_KC), jnp.int32),
                        pltpu.VMEM((IDX_HEADS, tq, LANES), F32),
                        pltpu.VMEM((tq, IDX_KC), jnp.int32)],
        compiler_params=_cparams(("parallel", "arbitrary")),
        name="indexer",
    )(qih, kit, small, qpos_col, kpos_rows)


def _attn_kernel(qtab_ref, ktab_ref, qt_ref, ka_ref, vt_ref, mk_ref, kpos_ref, qpos_ref, corr_ref,
                 o_ref, bias_ref, s_ref, smax_ref, *state_refs):
    s_id = pl.program_id(1)
    qi = qtab_ref[s_id]
    ki = ktab_ref[s_id]
    tq = qt_ref.shape[3] // 2
    tk = ka_ref.shape[2]
    dh = ATT_HEAD_DIM
    n_pairs = ATT_HEADS // 2
    kmax = ((qi + 1) * tq - 1) // tk
    m_refs, acc_refs = state_refs[:n_pairs], state_refs[n_pairs:]

    @pl.when(ki == 0)
    def _():
        for j in range(n_pairs):
            m_refs[j][...] = jnp.full(m_refs[j].shape, -jnp.inf, F32)
            acc_refs[j][...] = jnp.zeros(acc_refs[j].shape, F32)

    def step(diag):
        bias = mk_ref[0].astype(F32)
        bias_ref[:, 0:tq] = bias
        bias_ref[:, tq:2 * tq] = bias
        if diag:
            later = jnp.maximum(kpos_ref[0][:, 0:1] - qpos_ref[0], 0).astype(F32)
            later = jnp.concatenate([later, later], axis=1)

        def scores(j, slot):
            s = jnp.dot(ka_ref[0, (2 * j) // ATT_RQ], qt_ref[0, j],
                        preferred_element_type=F32) + bias_ref[...]
            if diag:
                s = s - corr_ref[j] * later
            s_ref[slot] = s
            smax_ref[slot] = jnp.max(s, axis=0, keepdims=True)

        for j in range(ATT_LOOKAHEAD):
            scores(j, j)
        for j in range(n_pairs):
            if j + ATT_LOOKAHEAD < n_pairs:
                scores(j + ATT_LOOKAHEAD, (j + ATT_LOOKAHEAD) % ATT_SLOTS)
            m_old = m_refs[j][...]
            m_new = jnp.maximum(m_old, smax_ref[j % ATT_SLOTS])
            p = jnp.exp2(s_ref[j % ATT_SLOTS] - m_new).astype(BF16)
            acc_refs[j][...] = acc_refs[j][...] * jnp.exp2(m_old - m_new) + jnp.dot(
                vt_ref[0, (2 * j) // ATT_RQ], p, preferred_element_type=F32)
            m_refs[j][...] = m_new

    is_diag = (ki + 1) * tk - 1 > qi * tq

    @pl.when(is_diag)
    def _():
        step(True)

    @pl.when(jnp.logical_not(is_diag))
    def _():
        step(False)

    @pl.when(ki == kmax)
    def _():
        for h in range(ATT_HEADS):
            lanes = slice((h % 2) * tq, (h % 2 + 1) * tq)
            out = acc_refs[h // 2][0:dh, lanes] / acc_refs[h // 2][dh:dh + 1, lanes]
            o_ref[0, :, h * dh:(h + 1) * dh] = jnp.transpose(out).astype(BF16)


def _attn(qt, ka, vt, mask_t, kpos_col, qpos_row):
    bsz, L = mask_t.shape[0], mask_t.shape[1]
    tq, tk = min(ATT_TQ, L), min(ATT_TK, L)
    assert qt.shape[3] == 2 * L and min(PREP_T, L) == tq
    n_pairs = ATT_HEADS // 2
    pairs = [(i, k) for i in range(L // tq) for k in range(((i + 1) * tq - 1) // tk + 1)]
    qtab = jnp.asarray(np.array([p[0] for p in pairs], np.int32))
    ktab = jnp.asarray(np.array([p[1] for p in pairs], np.int32))
    corr = jnp.broadcast_to(_alibi_corr().reshape(n_pairs, 1, 2, 1), (n_pairs, 1, 2, tq)).reshape(
        n_pairs, 1, 2 * tq)
    grid_spec = pltpu.PrefetchScalarGridSpec(
        num_scalar_prefetch=2,
        grid=(bsz, len(pairs)),
        in_specs=[pl.BlockSpec((1, n_pairs, QK_AUG, 2 * tq), lambda b, s, qt_, kt_: (b, 0, 0, qt_[s])),
                  pl.BlockSpec((1, ATT_KV_HEADS, tk, QK_AUG), lambda b, s, qt_, kt_: (b, 0, kt_[s], 0)),
                  pl.BlockSpec((1, ATT_KV_HEADS, VT_ROWS, tk), lambda b, s, qt_, kt_: (b, 0, 0, kt_[s])),
                  pl.BlockSpec((1, tk, tq), lambda b, s, qt_, kt_: (b, kt_[s], qt_[s])),
                  pl.BlockSpec((1, tk, LANES), lambda b, s, qt_, kt_: (b, kt_[s], 0)),
                  pl.BlockSpec((1, 1, tq), lambda b, s, qt_, kt_: (b, 0, qt_[s])),
                  pl.BlockSpec((n_pairs, 1, 2 * tq), lambda b, s, qt_, kt_: (0, 0, 0))],
        out_specs=pl.BlockSpec((1, tq, ATT_WIDTH), lambda b, s, qt_, kt_: (b, qt_[s], 0)),
        scratch_shapes=([pltpu.VMEM((tk, 2 * tq), F32), pltpu.VMEM((ATT_SLOTS, tk, 2 * tq), F32),
                         pltpu.VMEM((ATT_SLOTS, 1, 2 * tq), F32)]
                        + [pltpu.VMEM((1, 2 * tq), F32) for _ in range(n_pairs)]
                        + [pltpu.VMEM((VT_ROWS, 2 * tq), F32) for _ in range(n_pairs)]))
    return pl.pallas_call(
        _attn_kernel,
        out_shape=jax.ShapeDtypeStruct((bsz, L, ATT_WIDTH), BF16),
        grid_spec=grid_spec,
        compiler_params=_cparams(("parallel", "arbitrary")),
        name="attn",
    )(qtab, ktab, qt, ka, vt, mask_t, kpos_col, qpos_row, corr)


def _merge_kernel(ys_ref, ya_ref, g0_ref, g1_ref, ws_ref, wa_ref, o_ref):
    ps = jnp.dot(ys_ref[0], ws_ref[...], preferred_element_type=F32)
    pa = jnp.dot(ya_ref[0], wa_ref[...], preferred_element_type=F32)
    g0 = _sigmoid(g0_ref[0].astype(F32))
    g1 = _sigmoid(g1_ref[0].astype(F32))
    o_ref[0] = (g0 * ps + g1 * pa).astype(BF16)


def _merge(y_ssd, y_att, u, w_s, w_a):
    bsz, L, _ = y_ssd.shape
    d = w_s.shape[1]
    tm, tn = min(MRG_TM, L), min(MRG_TN, d)
    g0b = OFF_GATE // tn
    g1b = (OFF_GATE + d) // tn
    return pl.pallas_call(
        _merge_kernel,
        out_shape=jax.ShapeDtypeStruct((bsz, L, d), BF16),
        grid=(bsz, L // tm, d // tn),
        in_specs=[pl.BlockSpec((1, tm, SSD_D_INNER), lambda b, i, j: (b, i, 0)),
                  pl.BlockSpec((1, tm, ATT_WIDTH), lambda b, i, j: (b, i, 0)),
                  pl.BlockSpec((1, tm, tn), lambda b, i, j: (b, i, g0b + j)),
                  pl.BlockSpec((1, tm, tn), lambda b, i, j: (b, i, g1b + j)),
                  pl.BlockSpec((SSD_D_INNER, tn), lambda b, i, j: (0, j)),
                  pl.BlockSpec((ATT_WIDTH, tn), lambda b, i, j: (0, j))],
        out_specs=pl.BlockSpec((1, tm, tn), lambda b, i, j: (b, i, j)),
        compiler_params=_cparams(("parallel", "parallel", "arbitrary")),
        name="merge",
    )(y_ssd, y_att, u, u, w_s, w_a)


def _outproj_kernel(m_ref, h_ref, g_ref, w_ref, o_ref):
    o_ref[0] = h_ref[0] + g_ref[0] * jnp.dot(m_ref[0], w_ref[...], preferred_element_type=F32)


def _outproj(merged, h, gate, w_out):
    bsz, L, d = h.shape
    tm, tn = min(MRG_TM, L), min(MRG_TN, d)
    return pl.pallas_call(
        _outproj_kernel,
        out_shape=jax.ShapeDtypeStruct((bsz, L, d), F32),
        grid=(bsz, L // tm, d // tn),
        in_specs=[pl.BlockSpec((1, tm, d), lambda b, i, j: (b, i, 0)),
                  pl.BlockSpec((1, tm, tn), lambda b, i, j: (b, i, j)),
                  pl.BlockSpec((1, 1, tn), lambda b, i, j: (b, 0, j)),
                  pl.BlockSpec((d, tn), lambda b, i, j: (0, j))],
        out_specs=pl.BlockSpec((1, tm, tn), lambda b, i, j: (b, i, j)),
        compiler_params=_cparams(("parallel", "parallel", "arbitrary")),
        name="outproj",
    )(merged, h, gate, w_out)


def _split_w_in(w_in, d):
    sizes = (SSD_D_INNER, SSD_D_INNER + 2 * SSD_GROUPS * SSD_STATE, SSD_HEADS, ATT_WIDTH, ATT_KV_WIDTH,
             ATT_KV_WIDTH, IDX_HEADS * IDX_HEAD_DIM, IDX_HEAD_DIM, IDX_HEADS, 2 * d)
    offs = np.concatenate([[0], np.cumsum(sizes)])
    seg = [w_in[:, int(offs[i]):int(offs[i + 1])] for i in range(len(sizes))]
    w_z, w_xbc, w_dt, w_q, w_k, w_v, w_qi, w_ki, w_wi, w_g = seg
    w_main = jnp.concatenate([w_z, w_xbc, w_q, w_k, w_v, w_qi, w_g], axis=1).astype(BF16)
    pad = jnp.zeros((w_in.shape[0], LANES - (IDX_HEAD_DIM + IDX_HEADS + SSD_HEADS)), w_in.dtype)
    w_small = jnp.concatenate([w_ki, w_wi, w_dt, pad], axis=1).astype(BF16)
    return w_main, w_small


def kernel(x, c, positions, w_ada, b_ada, norm_ffn1, ffn1_w1, ffn1_w3, ffn1_w2, norm_mix, w_in,
           ssd_conv_w, ssd_conv_b, ssd_dt_bias, ssd_a_log, ssd_d, ssd_norm, q_norm, k_norm,
           w_br_ssd, w_br_att, w_out, norm_ffn2, ffn2_w1, ffn2_w3, ffn2_w2):
    bsz, L, d = x.shape
    depth = w_ada.shape[0]
    top_k = min(TOPK_MAX, L // 4)
    G, R = SSD_GROUPS, SSD_R
    kpos_col = positions.reshape(bsz, L, 1)
    kpos_lanes = jnp.broadcast_to(kpos_col, (bsz, L, LANES))
    qpos_row = positions.reshape(bsz, 1, L)
    kpos_rows = positions.reshape(bsz, L // IDX_KC, 1, IDX_KC)

    def pad_rows(a):
        return jnp.pad(a.reshape(G, R), ((0, 0), (0, 8 - R))).reshape(G, 8, 1)

    h = x
    for l in range(depth):
        mod = _ada(c, w_ada[l], b_ada[l]).reshape(bsz, N_MOD, 1, d)
        sh1, sc1, g1, sh2, sc2, g2, sh3, sc3, g3 = [mod[:, i] for i in range(N_MOD)]

        h = _ffn(h, norm_ffn1[l].reshape(1, d), sh1, sc1, g1,
                 ffn1_w1[l].astype(BF16), ffn1_w3[l].astype(BF16), ffn1_w2[l].astype(BF16))

        w_main, w_small = _split_w_in(w_in[l], d)
        u, small = _inproj(h, norm_mix[l].reshape(1, d), sh2, sc2, w_main, w_small)

        dt_raw = small[:, :, SM_DT:SM_DT + SSD_HEADS].reshape(bsz, L, G, R)
        dt_rows = jnp.pad(jnp.transpose(dt_raw, (0, 2, 3, 1)), ((0, 0), (0, 0), (0, 8 - R), (0, 0)))
        y_ssd = _ssd(u, dt_rows, ssd_conv_w[l], ssd_conv_b[l].reshape(1, -1),
                     pad_rows(ssd_dt_bias[l]), pad_rows(ssd_a_log[l]),
                     jnp.repeat(ssd_d[l], SSD_HEAD_DIM).reshape(1, SSD_D_INNER),
                     ssd_norm[l].reshape(1, SSD_D_INNER))

        qt, ka, vt, qih, kit = _dsa_prep(u, small, kpos_col, q_norm[l], k_norm[l])
        mask_t = _indexer(qih, kit, small, kpos_col, kpos_rows, top_k)
        y_att = _attn(qt, ka, vt, mask_t, kpos_lanes, qpos_row)

        merged = _merge(y_ssd, y_att, u, w_br_ssd[l].astype(BF16), w_br_att[l].astype(BF16))
        h = _outproj(merged, h, g2, w_out[l].astype(BF16))

        h = _ffn(h, norm_ffn2[l].reshape(1, d), sh3, sc3, g3,
                 ffn2_w1[l].astype(BF16), ffn2_w3[l].astype(BF16), ffn2_w2[l].astype(BF16))
    return h
```

```python
import functools
import math

import numpy as np
import jax
import jax.numpy as jnp
from jax import lax
from jax.experimental import pallas as pl
from jax.experimental.pallas import tpu as pltpu

F32 = jnp.float32
BF16 = jnp.bfloat16

CHUNK = 64
EPS = 1e-6
N_MOD = 9
SSD_D_INNER = 2048
SSD_HEAD_DIM = 64
SSD_GROUPS = 8
SSD_HEADS = SSD_D_INNER // SSD_HEAD_DIM
SSD_R = SSD_HEADS // SSD_GROUPS
SSD_GW = SSD_D_INNER // SSD_GROUPS
SSD_STATE = 128
SSD_CONV = 4
ATT_HEADS = 16
ATT_KV_HEADS = 4
ATT_RQ = ATT_HEADS // ATT_KV_HEADS
ATT_HEAD_DIM = 128
ATT_WIDTH = ATT_HEADS * ATT_HEAD_DIM
ATT_KV_WIDTH = ATT_KV_HEADS * ATT_HEAD_DIM
IDX_HEADS = 16
IDX_HEAD_DIM = 64
TOPK_MAX = 256
ALIBI_MAX_BIAS = 8.0
LOG2E = 1.4426950408889634
NEG_BIG = -1e30

LANES = 128
QK_AUG = 256
VT_ROWS = ATT_HEAD_DIM + 16
VMEM_LIMIT = 56 * 1024 * 1024

OFF_Z = 0
OFF_XBC = OFF_Z + SSD_D_INNER
OFF_Q = OFF_XBC + SSD_D_INNER + 2 * SSD_GROUPS * SSD_STATE
OFF_K = OFF_Q + ATT_WIDTH
OFF_V = OFF_K + ATT_KV_WIDTH
OFF_QI = OFF_V + ATT_KV_WIDTH
OFF_GATE = OFF_QI + IDX_HEADS * IDX_HEAD_DIM
SM_KIDX = 0
SM_WIDX = IDX_HEAD_DIM
SM_DT = SM_WIDX + IDX_HEADS

FFN_TM, FFN_TF = 512, 512
NORM_RB = 128
INP_TM, INP_TN = 1024, 1024
SSD_Q = 256
PREP_T = 256
IDX_TQ, IDX_KC, IDX_RB = 128, 256, 16
ATT_TQ, ATT_TK = 256, 512
ATT_LOOKAHEAD = 2
ATT_SLOTS = ATT_LOOKAHEAD + 1
MRG_TM, MRG_TN = 1024, 512


def _cparams(sem):
    return pltpu.CompilerParams(dimension_semantics=sem, vmem_limit_bytes=VMEM_LIMIT)


def _sigmoid(x):
    return 1.0 / (1.0 + jnp.exp(-x))


def _rms_mod(x, nw, sh, sc):
    ms = jnp.mean(x * x, axis=-1, keepdims=True)
    return (x * lax.rsqrt(ms + EPS) * nw) * (1.0 + sc) + sh


def _rms_mod_rows(x_ref, nw_ref, sh_ref, sc_ref, hn_ref):
    tm = hn_ref.shape[0]
    rb = min(NORM_RB, tm)

    def body(i, carry):
        rows = pl.ds(pl.multiple_of(i * rb, rb), rb)
        hn_ref[rows, :] = _rms_mod(x_ref[0, rows, :], nw_ref[...], sh_ref[0], sc_ref[0]).astype(BF16)
        return carry

    lax.fori_loop(0, tm // rb, body, 0)


def _ada_kernel(ct_ref, w_ref, b_ref, o_ref):
    ct = ct_ref[...]
    ca = ct * _sigmoid(ct)
    w = w_ref[...]
    rows = [jnp.sum(w * ca[:, b:b + 1], axis=0, keepdims=True) for b in range(ct.shape[1])]
    o_ref[...] = jnp.concatenate(rows, axis=0) + b_ref[...]


def _ada(c, w_ada, b_ada):
    bsz, d = c.shape
    n = w_ada.shape[1]
    tn = 1024 if n % 1024 == 0 else n
    return pl.pallas_call(
        _ada_kernel,
        out_shape=jax.ShapeDtypeStruct((bsz, n), F32),
        grid=(n // tn,),
        in_specs=[pl.BlockSpec((d, bsz), lambda j: (0, 0)),
                  pl.BlockSpec((d, tn), lambda j: (0, j)),
                  pl.BlockSpec((1, tn), lambda j: (0, j))],
        out_specs=pl.BlockSpec((bsz, tn), lambda j: (0, j)),
        compiler_params=_cparams(("arbitrary",)),
        name="ada",
    )(c.T, w_ada, b_ada.reshape(1, n))


def _ffn_kernel(x_ref, nw_ref, sh_ref, sc_ref, g_ref, w1_ref, w3_ref, w2_ref, o_ref, hn_ref):
    f = pl.program_id(2)

    @pl.when(f == 0)
    def _():
        _rms_mod_rows(x_ref, nw_ref, sh_ref, sc_ref, hn_ref)
        o_ref[0] = jnp.zeros(o_ref.shape[1:], F32)

    hn = hn_ref[...]
    a = jnp.dot(hn, w1_ref[...], preferred_element_type=F32)
    b = jnp.dot(hn, w3_ref[...], preferred_element_type=F32)
    g = (a * _sigmoid(a) * b).astype(BF16)
    o_ref[0] += jnp.dot(g, w2_ref[...], preferred_element_type=F32)

    @pl.when(f == pl.num_programs(2) - 1)
    def _():
        o_ref[0] = x_ref[0] + 0.5 * g_ref[0] * o_ref[0]


def _ffn(h, nw, sh, sc, gate, w1, w3, w2):
    bsz, L, d = h.shape
    ff = w1.shape[1]
    tm, tf = min(FFN_TM, L), min(FFN_TF, ff)
    vec = pl.BlockSpec((1, 1, d), lambda b, i, f: (b, 0, 0))
    return pl.pallas_call(
        _ffn_kernel,
        out_shape=jax.ShapeDtypeStruct((bsz, L, d), F32),
        grid=(bsz, L // tm, ff // tf),
        in_specs=[pl.BlockSpec((1, tm, d), lambda b, i, f: (b, i, 0)),
                  pl.BlockSpec((1, d), lambda b, i, f: (0, 0)),
                  vec, vec, vec,
                  pl.BlockSpec((d, tf), lambda b, i, f: (0, f)),
                  pl.BlockSpec((d, tf), lambda b, i, f: (0, f)),
                  pl.BlockSpec((tf, d), lambda b, i, f: (f, 0))],
        out_specs=pl.BlockSpec((1, tm, d), lambda b, i, f: (b, i, 0)),
        scratch_shapes=[pltpu.VMEM((tm, d), BF16)],
        compiler_params=_cparams(("parallel", "parallel", "arbitrary")),
        name="ffn",
    )(h, nw, sh, sc, gate, w1, w3, w2)


def _inproj_kernel(x_ref, nw_ref, sh_ref, sc_ref, w_ref, ws_ref, u_ref, s_ref, hn_ref):
    j = pl.program_id(2)

    @pl.when(j == 0)
    def _():
        _rms_mod_rows(x_ref, nw_ref, sh_ref, sc_ref, hn_ref)
        s_ref[0] = jnp.dot(hn_ref[...], ws_ref[...], preferred_element_type=F32)

    u_ref[0] = jnp.dot(hn_ref[...], w_ref[...], preferred_element_type=F32).astype(BF16)


def _inproj(h, nw, sh, sc, w_main, w_small):
    bsz, L, d = h.shape
    n = w_main.shape[1]
    tm = min(INP_TM, L)
    tn = INP_TN if n % INP_TN == 0 else INP_TN // 2
    vec = pl.BlockSpec((1, 1, d), lambda b, i, j: (b, 0, 0))
    return pl.pallas_call(
        _inproj_kernel,
        out_shape=(jax.ShapeDtypeStruct((bsz, L, n), BF16),
                   jax.ShapeDtypeStruct((bsz, L, LANES), F32)),
        grid=(bsz, L // tm, n // tn),
        in_specs=[pl.BlockSpec((1, tm, d), lambda b, i, j: (b, i, 0)),
                  pl.BlockSpec((1, d), lambda b, i, j: (0, 0)),
                  vec, vec,
                  pl.BlockSpec((d, tn), lambda b, i, j: (0, j)),
                  pl.BlockSpec((d, LANES), lambda b, i, j: (0, 0))],
        out_specs=(pl.BlockSpec((1, tm, tn), lambda b, i, j: (b, i, j)),
                   pl.BlockSpec((1, tm, LANES), lambda b, i, j: (b, i, 0))),
        scratch_shapes=[pltpu.VMEM((tm, d), BF16)],
        compiler_params=_cparams(("parallel", "parallel", "arbitrary")),
        name="inproj",
    )(h, nw, sh, sc, w_main, w_small)


def _ssd_kernel(xs_ref, bm_ref, cm_ref, z_ref, dtr_ref, cwx_ref, cwb_ref, cwc_ref,
                cbx_ref, cbb_ref, cbc_ref, dtb_ref, alog_ref, dsk_ref, nw_ref,
                o_ref, ext_ref, state_ref):
    t = pl.program_id(2)
    q = xs_ref.shape[1]
    gw, ns = SSD_GW, SSD_STATE
    cw = gw + 2 * ns

    @pl.when(t == 0)
    def _():
        ext_ref[0:8, :] = jnp.zeros((8, cw), F32)
        state_ref[...] = jnp.zeros(state_ref.shape, F32)

    ext_ref[8:8 + q, 0:gw] = xs_ref[0].astype(F32)
    ext_ref[8:8 + q, gw:gw + ns] = bm_ref[0].astype(F32)
    ext_ref[8:8 + q, gw + ns:cw] = cm_ref[0].astype(F32)
    wts = jnp.concatenate([cwx_ref[...], cwb_ref[...], cwc_ref[...]], axis=1)
    acc = jnp.concatenate([cbx_ref[...], cbb_ref[...], cbc_ref[...]], axis=1)
    for j in range(SSD_CONV):
        acc = acc + wts[j:j + 1, :] * ext_ref[8 - (SSD_CONV - 1) + j:8 - (SSD_CONV - 1) + j + q, :]
    tail = ext_ref[q:q + 8, :]
    ext_ref[0:8, :] = tail
    xc = acc * _sigmoid(acc)
    xs = xc[:, 0:gw]
    bm = xc[:, gw:gw + ns].astype(BF16)
    cm = xc[:, gw + ns:cw].astype(BF16)
    xs_b = xs.astype(BF16)

    dtx = dtr_ref[0, 0] + dtb_ref[0]
    dt_row = jnp.maximum(dtx, 0.0) + jnp.log(1.0 + jnp.exp(-jnp.abs(dtx)))
    dta_row = dt_row * (-jnp.exp(alog_ref[0]))

    ti = lax.broadcasted_iota(jnp.int32, (q, q), 0)
    si = lax.broadcasted_iota(jnp.int32, (q, q), 1)
    tril = si <= ti
    eye = si == ti
    lane_head = lax.broadcasted_iota(jnp.int32, (1, gw), 1) // SSD_HEAD_DIM

    cb = lax.dot_general(cm, bm, (((1,), (1,)), ((), ())), preferred_element_type=F32)

    y = jnp.zeros((q, gw), F32)
    f_exp = jnp.zeros((q, gw), F32)
    f_w = jnp.zeros((q, gw), F32)
    e_dec = jnp.zeros((1, gw), F32)
    for r in range(SSD_R):
        dta_r = dta_row[r:r + 1, :]
        dt_r = dt_row[r:r + 1, :]
        acum_c = jnp.sum(jnp.where(tril, dta_r, 0.0), axis=1, keepdims=True)
        acum_r = jnp.sum(jnp.where(eye, acum_c, 0.0), axis=0, keepdims=True)
        dt_c = jnp.sum(jnp.where(eye, dt_r, 0.0), axis=1, keepdims=True)
        decay = jnp.exp(jnp.where(tril, acum_c - acum_r, -jnp.inf))
        m_r = (cb * decay * dt_r).astype(BF16)
        yd = jnp.dot(m_r, xs_b, preferred_element_type=F32)
        sel = lane_head == r
        y = y + jnp.where(sel, yd, 0.0)
        a_last = acum_r[:, q - 1:q]
        f_exp = f_exp + jnp.where(sel, jnp.exp(acum_c), 0.0)
        f_w = f_w + jnp.where(sel, dt_c * jnp.exp(a_last - acum_c), 0.0)
        e_dec = e_dec + jnp.where(sel, jnp.exp(a_last), 0.0)

    state = state_ref[...]
    y = y + jnp.dot(cm, state.astype(BF16), preferred_element_type=F32) * f_exp
    xw = (xs * f_w).astype(BF16)
    bm_t = jnp.transpose(xc[:, gw:gw + ns]).astype(BF16)
    state_ref[...] = state * e_dec + jnp.dot(bm_t, xw, preferred_element_type=F32)

    y = y + dsk_ref[...] * xs
    zf = z_ref[0].astype(F32)
    y = y * (zf * _sigmoid(zf))
    ms = jnp.mean(y * y, axis=-1, keepdims=True)
    o_ref[0] = (y * lax.rsqrt(ms + EPS) * nw_ref[...]).astype(BF16)


def _ssd(u, dt_rows, conv_w, conv_b, dt_bias, a_log, d_skip, norm_w):
    bsz, L, _ = u.shape
    q = min(SSD_Q, L)
    gw, ns, G = SSD_GW, SSD_STATE, SSD_GROUPS
    xb = OFF_XBC // gw
    bb = (OFF_XBC + SSD_D_INNER) // ns
    cb = bb + G
    cwb0 = SSD_D_INNER // ns
    return pl.pallas_call(
        _ssd_kernel,
        out_shape=jax.ShapeDtypeStruct((bsz, L, SSD_D_INNER), BF16),
        grid=(bsz, G, L // q),
        in_specs=[pl.BlockSpec((1, q, gw), lambda b, g, t: (b, t, xb + g)),
                  pl.BlockSpec((1, q, ns), lambda b, g, t: (b, t, bb + g)),
                  pl.BlockSpec((1, q, ns), lambda b, g, t: (b, t, cb + g)),
                  pl.BlockSpec((1, q, gw), lambda b, g, t: (b, t, g)),
                  pl.BlockSpec((1, 1, 8, q), lambda b, g, t: (b, g, 0, t)),
                  pl.BlockSpec((SSD_CONV, gw), lambda b, g, t: (0, g)),
                  pl.BlockSpec((SSD_CONV, ns), lambda b, g, t: (0, cwb0 + g)),
                  pl.BlockSpec((SSD_CONV, ns), lambda b, g, t: (0, cwb0 + G + g)),
                  pl.BlockSpec((1, gw), lambda b, g, t: (0, g)),
                  pl.BlockSpec((1, ns), lambda b, g, t: (0, cwb0 + g)),
                  pl.BlockSpec((1, ns), lambda b, g, t: (0, cwb0 + G + g)),
                  pl.BlockSpec((1, 8, 1), lambda b, g, t: (g, 0, 0)),
                  pl.BlockSpec((1, 8, 1), lambda b, g, t: (g, 0, 0)),
                  pl.BlockSpec((1, gw), lambda b, g, t: (0, g)),
                  pl.BlockSpec((1, gw), lambda b, g, t: (0, g))],
        out_specs=pl.BlockSpec((1, q, gw), lambda b, g, t: (b, t, g)),
        scratch_shapes=[pltpu.VMEM((q + 8, gw + 2 * ns), F32),
                        pltpu.VMEM((ns, gw), F32)],
        compiler_params=_cparams(("parallel", "parallel", "arbitrary")),
        name="ssd",
    )(u, u, u, u, dt_rows, conv_w, conv_w, conv_w, conv_b, conv_b, conv_b,
      dt_bias, a_log, d_skip, norm_w)


def _prep_kernel(q_ref, k_ref, v_ref, qi_ref, sm_ref, kpos_ref, qnw_ref, knw_ref, aug_ref,
                 qt_ref, ka_ref, vt_ref, qih_ref, kit_ref):
    tp = q_ref.shape[1]
    dh = ATT_HEAD_DIM
    qscale = (ATT_HEAD_DIM ** -0.5) * LOG2E

    def rms(xh, w):
        return xh * lax.rsqrt(jnp.mean(xh * xh, axis=-1, keepdims=True) + EPS) * w

    qf = q_ref[0].astype(F32)
    for h in range(ATT_HEADS):
        qh = rms(qf[:, h * dh:(h + 1) * dh], qnw_ref[...]) * qscale
        lanes = slice((h % 2) * tp, (h % 2 + 1) * tp)
        qt_ref[0, h // 2, 0:dh, lanes] = jnp.transpose(qh).astype(BF16)
        qt_ref[0, h // 2, dh:QK_AUG, lanes] = jnp.broadcast_to(aug_ref[h], (QK_AUG - dh, tp)).astype(BF16)

    kp = kpos_ref[0]
    lane = lax.broadcasted_iota(jnp.int32, (tp, QK_AUG - dh), 1)
    p_hi = (kp >> 7).astype(F32)
    p_lo = (kp & 127).astype(F32)
    pos_cols = jnp.where(lane < 3, p_hi, jnp.where(lane < 6, p_lo, 0.0)).astype(BF16)
    kf = k_ref[0].astype(F32)
    vf = v_ref[0].astype(F32)
    for g in range(ATT_KV_HEADS):
        ka_ref[0, g, :, 0:dh] = rms(kf[:, g * dh:(g + 1) * dh], knw_ref[...]).astype(BF16)
        ka_ref[0, g, :, dh:QK_AUG] = pos_cols
        vt_ref[0, g, 0:dh, :] = jnp.transpose(vf[:, g * dh:(g + 1) * dh]).astype(BF16)
        vt_ref[0, g, dh:VT_ROWS, :] = jnp.ones((VT_ROWS - dh, tp), BF16)

    qif = qi_ref[0].astype(F32)
    for h in range(IDX_HEADS):
        qh = qif[:, h * IDX_HEAD_DIM:(h + 1) * IDX_HEAD_DIM]
        qih_ref[0, :, h] = qh.reshape(tp // IDX_RB, IDX_RB, IDX_HEAD_DIM).astype(BF16)
    sm_t = jnp.transpose(sm_ref[0])
    for j in range(tp // IDX_KC):
        kit_ref[0, j] = sm_t[SM_KIDX:SM_KIDX + IDX_HEAD_DIM, j * IDX_KC:(j + 1) * IDX_KC].astype(BF16)


def _alibi_aug():
    out = np.zeros((ATT_HEADS, QK_AUG - ATT_HEAD_DIM, 1), np.float32)
    for h in range(ATT_HEADS):
        s = np.float32(2.0 ** (-ALIBI_MAX_BIAS * (h + 1) / ATT_HEADS)) * np.float32(LOG2E)
        rem = np.float32(s)
        for i in range(3):
            piece = np.float32(np.asarray(rem, np.float32).astype(BF16))
            out[h, i, 0] = piece * np.float32(128.0)
            out[h, 3 + i, 0] = piece
            rem = np.float32(rem - piece)
    return jnp.asarray(out)


def _alibi_corr():
    s = [np.float32(2.0) * np.float32(2.0 ** (-ALIBI_MAX_BIAS * (h + 1) / ATT_HEADS)) * np.float32(LOG2E)
         for h in range(ATT_HEADS)]
    return jnp.asarray(np.array(s, np.float32).reshape(ATT_HEADS, 1, 1))


def _dsa_prep(u, small, kpos_col, q_norm, k_norm):
    bsz, L, _ = u.shape
    tp = min(PREP_T, L)
    nkc = L // IDX_KC
    return pl.pallas_call(
        _prep_kernel,
        out_shape=(jax.ShapeDtypeStruct((bsz, ATT_HEADS // 2, QK_AUG, 2 * L), BF16),
                   jax.ShapeDtypeStruct((bsz, ATT_KV_HEADS, L, QK_AUG), BF16),
                   jax.ShapeDtypeStruct((bsz, ATT_KV_HEADS, VT_ROWS, L), BF16),
                   jax.ShapeDtypeStruct((bsz, L // IDX_RB, IDX_HEADS, IDX_RB, IDX_HEAD_DIM), BF16),
                   jax.ShapeDtypeStruct((bsz, nkc, IDX_HEAD_DIM, IDX_KC), BF16)),
        grid=(bsz, L // tp),
        in_specs=[pl.BlockSpec((1, tp, ATT_WIDTH), lambda b, i: (b, i, OFF_Q // ATT_WIDTH)),
                  pl.BlockSpec((1, tp, ATT_KV_WIDTH), lambda b, i: (b, i, OFF_K // ATT_KV_WIDTH)),
                  pl.BlockSpec((1, tp, ATT_KV_WIDTH), lambda b, i: (b, i, OFF_V // ATT_KV_WIDTH)),
                  pl.BlockSpec((1, tp, IDX_HEADS * IDX_HEAD_DIM),
                               lambda b, i: (b, i, OFF_QI // (IDX_HEADS * IDX_HEAD_DIM))),
                  pl.BlockSpec((1, tp, LANES), lambda b, i: (b, i, 0)),
                  pl.BlockSpec((1, tp, 1), lambda b, i: (b, i, 0)),
                  pl.BlockSpec((1, ATT_HEAD_DIM), lambda b, i: (0, 0)),
                  pl.BlockSpec((1, ATT_HEAD_DIM), lambda b, i: (0, 0)),
                  pl.BlockSpec((ATT_HEADS, QK_AUG - ATT_HEAD_DIM, 1), lambda b, i: (0, 0, 0))],
        out_specs=(pl.BlockSpec((1, ATT_HEADS // 2, QK_AUG, 2 * tp), lambda b, i: (b, 0, 0, i)),
                   pl.BlockSpec((1, ATT_KV_HEADS, tp, QK_AUG), lambda b, i: (b, 0, i, 0)),
                   pl.BlockSpec((1, ATT_KV_HEADS, VT_ROWS, tp), lambda b, i: (b, 0, 0, i)),
                   pl.BlockSpec((1, tp // IDX_RB, IDX_HEADS, IDX_RB, IDX_HEAD_DIM),
                                lambda b, i: (b, i, 0, 0, 0)),
                   pl.BlockSpec((1, tp // IDX_KC, IDX_HEAD_DIM, IDX_KC), lambda b, i: (b, i, 0, 0))),
        compiler_params=_cparams(("parallel", "parallel")),
        name="dsa_prep",
    )(u, u, u, u, small, kpos_col, q_norm.reshape(1, -1), k_norm.reshape(1, -1), _alibi_aug())


_INT_MIN = -2 ** 31
_NEG_INF_KEY = -2139095041


def _tree_sum(terms):
    while len(terms) > 1:
        terms = [terms[i] + terms[i + 1] for i in range(0, len(terms) - 1, 2)] + (
            [terms[-1]] if len(terms) % 2 else [])
    return terms[0]


def _indexer_kernel(qih_ref, kit_ref, sm_ref, qpos_ref, kposr_ref, o_ref,
                    keys_ref, wb_ref, bmax_ref, *, top_k):
    qi = pl.program_id(1)
    nrb = qih_ref.shape[1]
    tq = nrb * IDX_RB
    kc_w = kit_ref.shape[3]
    nkc_all = kit_ref.shape[1]
    hw = kc_w // 2
    assert hw == LANES and top_k <= kc_w and nkc_all % 4 == 0
    n_kc = jnp.minimum(((qi + 1) * tq + kc_w - 1) // kc_w, nkc_all)
    idx_scale = (IDX_HEAD_DIM * IDX_HEADS) ** -0.5

    sm = sm_ref[0]
    for h in range(IDX_HEADS):
        wb_ref[h] = jnp.broadcast_to(sm[:, SM_WIDX + h:SM_WIDX + h + 1], (tq, LANES))
    bmax_ref[...] = jnp.full(bmax_ref.shape, _INT_MIN, jnp.int32)

    def score_chunk(kc):
        kt = kit_ref[0, kc]
        kchunk = kposr_ref[0, kc] >> 6
        for rb in range(nrb):
            rows = slice(rb * IDX_RB, (rb + 1) * IDX_RB)
            s = jnp.dot(qih_ref[0, rb].reshape(IDX_HEADS * IDX_RB, IDX_HEAD_DIM), kt,
                        preferred_element_type=F32)
            t0, t1 = [], []
            for h in range(IDX_HEADS):
                rel = jnp.maximum(s[h * IDX_RB:(h + 1) * IDX_RB, :], 0.0)
                w = wb_ref[h, rows, :]
                t0.append(w * rel[:, 0:hw])
                t1.append(w * rel[:, hw:kc_w])
            score = jnp.concatenate([_tree_sum(t0), _tree_sum(t1)], axis=1) * idx_scale
            qchunk = qpos_ref[0, rows, :] >> 6
            score = jnp.where(kchunk <= qchunk, score, -jnp.inf)
            bits = pltpu.bitcast(score, jnp.int32)
            key = bits ^ ((bits >> 31) & 0x7FFFFFFF)
            keys_ref[kc, rows, :] = key
            bmax_ref[rows, :] = jnp.maximum(bmax_ref[rows, :], key)

    n_pair = (n_kc + 1) // 2

    def pair_body(pair, carry):
        score_chunk(2 * pair)
        score_chunk(2 * pair + 1)
        return carry

    lax.fori_loop(0, n_pair, pair_body, 0)

    n_quad = (n_kc + 3) // 4

    def pad_body(kc, carry):
        keys_ref[kc] = jnp.full((tq, kc_w), _INT_MIN, jnp.int32)
        return carry

    lax.fori_loop(2 * n_pair, 4 * n_quad, pad_body, 0)

    bm = bmax_ref[...]
    lb = jnp.min(bm, axis=1, keepdims=True)
    ub = jnp.max(bm, axis=1, keepdims=True)
    span = ub - lb
    wrapped = span < 0
    lb = jnp.where(wrapped, _INT_MIN, lb)
    nb = jnp.where(wrapped, 32, 32 - lax.clz(span))
    nbits = jnp.max(nb)

    def bit_body(i, thr):
        cand = thr + jnp.left_shift(jnp.int32(1), nbits - 1 - i)
        candb = jnp.broadcast_to(cand, (tq, hw))

        def cnt_body(quad, acc):
            for kc in range(4):
                k = keys_ref[4 * quad + kc]
                acc = acc + jnp.where(k[:, 0:hw] >= candb, 1, 0) + jnp.where(k[:, hw:kc_w] >= candb, 1, 0)
            return acc

        acc = lax.fori_loop(0, n_quad, cnt_body, jnp.zeros((tq, hw), jnp.int32))
        cnt = jnp.sum(acc, axis=1, keepdims=True)
        return jnp.where(jnp.logical_and(cnt >= top_k, cand > thr), cand, thr)

    thr = lax.fori_loop(0, nbits, bit_body, lb)
    thr = jnp.maximum(thr, _NEG_INF_KEY + 1)
    thr_b = jnp.broadcast_to(thr, (tq, hw))
    thr_full = jnp.concatenate([thr_b] * (kc_w // hw), axis=1)

    def mask_body(kc, carry):
        m = jnp.where(keys_ref[kc] >= thr_full, 0.0, NEG_BIG).astype(BF16)
        r0 = pl.multiple_of(kc * kc_w, kc_w)
        o_ref[0, pl.ds(r0, kc_w), :] = jnp.transpose(m)
        return carry

    lax.fori_loop(0, n_kc, mask_body, 0)

    def fill_body(kc, carry):
        r0 = pl.multiple_of(kc * kc_w, kc_w)
        o_ref[0, pl.ds(r0, kc_w), :] = jnp.full((kc_w, tq), NEG_BIG, BF16)
        return carry

    lax.fori_loop(n_kc, nkc_all, fill_body, 0)


def _indexer(qih, kit, small, qpos_col, kpos_rows, top_k):
    bsz, L = small.shape[0], small.shape[1]
    tq = min(IDX_TQ, L)
    nkc = L // IDX_KC
    return pl.pallas_call(
        functools.partial(_indexer_kernel, top_k=top_k),
        out_shape=jax.ShapeDtypeStruct((bsz, L, L), BF16),
        grid=(bsz, L // tq),
        in_specs=[pl.BlockSpec((1, tq // IDX_RB, IDX_HEADS, IDX_RB, IDX_HEAD_DIM),
                               lambda b, i: (b, i, 0, 0, 0)),
                  pl.BlockSpec((1, nkc, IDX_HEAD_DIM, IDX_KC), lambda b, i: (b, 0, 0, 0)),
                  pl.BlockSpec((1, tq, LANES), lambda b, i: (b, i, 0)),
                  pl.BlockSpec((1, tq, 1), lambda b, i: (b, i, 0)),
                  pl.BlockSpec((1, nkc, 1, IDX_KC), lambda b, i: (b, 0, 0, 0))],
        out_specs=pl.BlockSpec((1, L, tq), lambda b, i: (b, 0, i)),
        scratch_shapes=[pltpu.VMEM((nkc, tq, IDX_KC), jnp.int32),
                        pltpu.VMEM((IDX_HEADS, tq, LANES), F32),
                        pltpu.VMEM((tq, IDX_KC), jnp.int32)],
        compiler_params=_cparams(("parallel", "arbitrary")),
        name="indexer",
    )(qih, kit, small, qpos_col, kpos_rows)


def _attn_kernel(qtab_ref, ktab_ref, qt_ref, ka_ref, vt_ref, mk_ref, kpos_ref, qpos_ref, corr_ref,
                 o_ref, bias_ref, s_ref, smax_ref, *state_refs):
    s_id = pl.program_id(1)
    qi = qtab_ref[s_id]
    ki = ktab_ref[s_id]
    tq = qt_ref.shape[3] // 2
    tk = ka_ref.shape[2]
    dh = ATT_HEAD_DIM
    n_pairs = ATT_HEADS // 2
    kmax = ((qi + 1) * tq - 1) // tk
    m_refs, acc_refs = state_refs[:n_pairs], state_refs[n_pairs:]

    @pl.when(ki == 0)
    def _():
        for j in range(n_pairs):
            m_refs[j][...] = jnp.full(m_refs[j].shape, -jnp.inf, F32)
            acc_refs[j][...] = jnp.zeros(acc_refs[j].shape, F32)

    def step(diag):
        bias = mk_ref[0].astype(F32)
        bias_ref[:, 0:tq] = bias
        bias_ref[:, tq:2 * tq] = bias
        if diag:
            later = jnp.maximum(kpos_ref[0][:, 0:1] - qpos_ref[0], 0).astype(F32)
            later = jnp.concatenate([later, later], axis=1)

        def scores(j, slot):
            s = jnp.dot(ka_ref[0, (2 * j) // ATT_RQ], qt_ref[0, j],
                        preferred_element_type=F32) + bias_ref[...]
            if diag:
                s = s - corr_ref[j] * later
            s_ref[slot] = s
            smax_ref[slot] = jnp.max(s, axis=0, keepdims=True)

        for j in range(ATT_LOOKAHEAD):
            scores(j, j)
        for j in range(n_pairs):
            if j + ATT_LOOKAHEAD < n_pairs:
                scores(j + ATT_LOOKAHEAD, (j + ATT_LOOKAHEAD) % ATT_SLOTS)
            m_old = m_refs[j][...]
            m_new = jnp.maximum(m_old, smax_ref[j % ATT_SLOTS])
            p = jnp.exp2(s_ref[j % ATT_SLOTS] - m_new).astype(BF16)
            acc_refs[j][...] = acc_refs[j][...] * jnp.exp2(m_old - m_new) + jnp.dot(
                vt_ref[0, (2 * j) // ATT_RQ], p, preferred_element_type=F32)
            m_refs[j][...] = m_new

    is_diag = (ki + 1) * tk - 1 > qi * tq

    @pl.when(is_diag)
    def _():
        step(True)

    @pl.when(jnp.logical_not(is_diag))
    def _():
        step(False)

    @pl.when(ki == kmax)
    def _():
        for h in range(ATT_HEADS):
            lanes = slice((h % 2) * tq, (h % 2 + 1) * tq)
            out = acc_refs[h // 2][0:dh, lanes] / acc_refs[h // 2][dh:dh + 1, lanes]
            o_ref[0, :, h * dh:(h + 1) * dh] = jnp.transpose(out).astype(BF16)


def _attn(qt, ka, vt, mask_t, kpos_col, qpos_row):
    bsz, L = mask_t.shape[0], mask_t.shape[1]
    tq, tk = min(ATT_TQ, L), min(ATT_TK, L)
    assert qt.shape[3] == 2 * L and min(PREP_T, L) == tq
    n_pairs = ATT_HEADS // 2
    pairs = [(i, k) for i in range(L // tq) for k in range(((i + 1) * tq - 1) // tk + 1)]
    qtab = jnp.asarray(np.array([p[0] for p in pairs], np.int32))
    ktab = jnp.asarray(np.array([p[1] for p in pairs], np.int32))
    corr = jnp.broadcast_to(_alibi_corr().reshape(n_pairs, 1, 2, 1), (n_pairs, 1, 2, tq)).reshape(
        n_pairs, 1, 2 * tq)
    grid_spec = pltpu.PrefetchScalarGridSpec(
        num_scalar_prefetch=2,
        grid=(bsz, len(pairs)),
        in_specs=[pl.BlockSpec((1, n_pairs, QK_AUG, 2 * tq), lambda b, s, qt_, kt_: (b, 0, 0, qt_[s])),
                  pl.BlockSpec((1, ATT_KV_HEADS, tk, QK_AUG), lambda b, s, qt_, kt_: (b, 0, kt_[s], 0)),
                  pl.BlockSpec((1, ATT_KV_HEADS, VT_ROWS, tk), lambda b, s, qt_, kt_: (b, 0, 0, kt_[s])),
                  pl.BlockSpec((1, tk, tq), lambda b, s, qt_, kt_: (b, kt_[s], qt_[s])),
                  pl.BlockSpec((1, tk, LANES), lambda b, s, qt_, kt_: (b, kt_[s], 0)),
                  pl.BlockSpec((1, 1, tq), lambda b, s, qt_, kt_: (b, 0, qt_[s])),
                  pl.BlockSpec((n_pairs, 1, 2 * tq), lambda b, s, qt_, kt_: (0, 0, 0))],
        out_specs=pl.BlockSpec((1, tq, ATT_WIDTH), lambda b, s, qt_, kt_: (b, qt_[s], 0)),
        scratch_shapes=([pltpu.VMEM((tk, 2 * tq), F32), pltpu.VMEM((ATT_SLOTS, tk, 2 * tq), F32),
                         pltpu.VMEM((ATT_SLOTS, 1, 2 * tq), F32)]
                        + [pltpu.VMEM((1, 2 * tq), F32) for _ in range(n_pairs)]
                        + [pltpu.VMEM((VT_ROWS, 2 * tq), F32) for _ in range(n_pairs)]))
    return pl.pallas_call(
        _attn_kernel,
        out_shape=jax.ShapeDtypeStruct((bsz, L, ATT_WIDTH), BF16),
        grid_spec=grid_spec,
        compiler_params=_cparams(("parallel", "arbitrary")),
        name="attn",
    )(qtab, ktab, qt, ka, vt, mask_t, kpos_col, qpos_row, corr)


def _merge_kernel(ys_ref, ya_ref, g0_ref, g1_ref, ws_ref, wa_ref, o_ref):
    ps = jnp.dot(ys_ref[0], ws_ref[...], preferred_element_type=F32)
    pa = jnp.dot(ya_ref[0], wa_ref[...], preferred_element_type=F32)
    g0 = _sigmoid(g0_ref[0].astype(F32))
    g1 = _sigmoid(g1_ref[0].astype(F32))
    o_ref[0] = (g0 * ps + g1 * pa).astype(BF16)


def _merge(y_ssd, y_att, u, w_s, w_a):
    bsz, L, _ = y_ssd.shape
    d = w_s.shape[1]
    tm, tn = min(MRG_TM, L), min(MRG_TN, d)
    g0b = OFF_GATE // tn
    g1b = (OFF_GATE + d) // tn
    return pl.pallas_call(
        _merge_kernel,
        out_shape=jax.ShapeDtypeStruct((bsz, L, d), BF16),
        grid=(bsz, L // tm, d // tn),
        in_specs=[pl.BlockSpec((1, tm, SSD_D_INNER), lambda b, i, j: (b, i, 0)),
                  pl.BlockSpec((1, tm, ATT_WIDTH), lambda b, i, j: (b, i, 0)),
                  pl.BlockSpec((1, tm, tn), lambda b, i, j: (b, i, g0b + j)),
                  pl.BlockSpec((1, tm, tn), lambda b, i, j: (b, i, g1b + j)),
                  pl.BlockSpec((SSD_D_INNER, tn), lambda b, i, j: (0, j)),
                  pl.BlockSpec((ATT_WIDTH, tn), lambda b, i, j: (0, j))],
        out_specs=pl.BlockSpec((1, tm, tn), lambda b, i, j: (b, i, j)),
        compiler_params=_cparams(("parallel", "parallel", "arbitrary")),
        name="merge",
    )(y_ssd, y_att, u, u, w_s, w_a)


def _outproj_kernel(m_ref, h_ref, g_ref, w_ref, o_ref):
    o_ref[0] = h_ref[0] + g_ref[0] * jnp.dot(m_ref[0], w_ref[...], preferred_element_type=F32)


def _outproj(merged, h, gate, w_out):
    bsz, L, d = h.shape
    tm, tn = min(MRG_TM, L), min(MRG_TN, d)
    return pl.pallas_call(
        _outproj_kernel,
        out_shape=jax.ShapeDtypeStruct((bsz, L, d), F32),
        grid=(bsz, L // tm, d // tn),
        in_specs=[pl.BlockSpec((1, tm, d), lambda b, i, j: (b, i, 0)),
                  pl.BlockSpec((1, tm, tn), lambda b, i, j: (b, i, j)),
                  pl.BlockSpec((1, 1, tn), lambda b, i, j: (b, 0, j)),
                  pl.BlockSpec((d, tn), lambda b, i, j: (0, j))],
        out_specs=pl.BlockSpec((1, tm, tn), lambda b, i, j: (b, i, j)),
        compiler_params=_cparams(("parallel", "parallel", "arbitrary")),
        name="outproj",
    )(merged, h, gate, w_out)


def _split_w_in(w_in, d):
    sizes = (SSD_D_INNER, SSD_D_INNER + 2 * SSD_GROUPS * SSD_STATE, SSD_HEADS, ATT_WIDTH, ATT_KV_WIDTH,
             ATT_KV_WIDTH, IDX_HEADS * IDX_HEAD_DIM, IDX_HEAD_DIM, IDX_HEADS, 2 * d)
    offs = np.concatenate([[0], np.cumsum(sizes)])
    seg = [w_in[:, int(offs[i]):int(offs[i + 1])] for i in range(len(sizes))]
    w_z, w_xbc, w_dt, w_q, w_k, w_v, w_qi, w_ki, w_wi, w_g = seg
    w_main = jnp.concatenate([w_z, w_xbc, w_q, w_k, w_v, w_qi, w_g], axis=1).astype(BF16)
    pad = jnp.zeros((w_in.shape[0], LANES - (IDX_HEAD_DIM + IDX_HEADS + SSD_HEADS)), w_in.dtype)
    w_small = jnp.concatenate([w_ki, w_wi, w_dt, pad], axis=1).astype(BF16)
    return w_main, w_small


def kernel(x, c, positions, w_ada, b_ada, norm_ffn1, ffn1_w1, ffn1_w3, ffn1_w2, norm_mix, w_in,
           ssd_conv_w, ssd_conv_b, ssd_dt_bias, ssd_a_log, ssd_d, ssd_norm, q_norm, k_norm,
           w_br_ssd, w_br_att, w_out, norm_ffn2, ffn2_w1, ffn2_w3, ffn2_w2):
    bsz, L, d = x.shape
    depth = w_ada.shape[0]
    top_k = min(TOPK_MAX, L // 4)
    G, R = SSD_GROUPS, SSD_R
    kpos_col = positions.reshape(bsz, L, 1)
    kpos_lanes = jnp.broadcast_to(kpos_col, (bsz, L, LANES))
    qpos_row = positions.reshape(bsz, 1, L)
    kpos_rows = positions.reshape(bsz, L // IDX_KC, 1, IDX_KC)

    def pad_rows(a):
        return jnp.pad(a.reshape(G, R), ((0, 0), (0, 8 - R))).reshape(G, 8, 1)

    h = x
    for l in range(depth):
        mod = _ada(c, w_ada[l], b_ada[l]).reshape(bsz, N_MOD, 1, d)
        sh1, sc1, g1, sh2, sc2, g2, sh3, sc3, g3 = [mod[:, i] for i in range(N_MOD)]

        h = _ffn(h, norm_ffn1[l].reshape(1, d), sh1, sc1, g1,
                 ffn1_w1[l].astype(BF16), ffn1_w3[l].astype(BF16), ffn1_w2[l].astype(BF16))

        w_main, w_small = _split_w_in(w_in[l], d)
        u, small = _inproj(h, norm_mix[l].reshape(1, d), sh2, sc2, w_main, w_small)

        dt_raw = small[:, :, SM_DT:SM_DT + SSD_HEADS].reshape(bsz, L, G, R)
        dt_rows = jnp.pad(jnp.transpose(dt_raw, (0, 2, 3, 1)), ((0, 0), (0, 0), (0, 8 - R), (0, 0)))
        y_ssd = _ssd(u, dt_rows, ssd_conv_w[l], ssd_conv_b[l].reshape(1, -1),
                     pad_rows(ssd_dt_bias[l]), pad_rows(ssd_a_log[l]),
                     jnp.repeat(ssd_d[l], SSD_HEAD_DIM).reshape(1, SSD_D_INNER),
                     ssd_norm[l].reshape(1, SSD_D_INNER))

        qt, ka, vt, qih, kit = _dsa_prep(u, small, kpos_col, q_norm[l], k_norm[l])
        mask_t = _indexer(qih, kit, small, kpos_col, kpos_rows, top_k)
        y_att = _attn(qt, ka, vt, mask_t, kpos_lanes, qpos_row)

        merged = _merge(y_ssd, y_att, u, w_br_ssd[l].astype(BF16), w_br_att[l].astype(BF16))
        h = _outproj(merged, h, g2, w_out[l].astype(BF16))

        h = _ffn(h, norm_ffn2[l].reshape(1, d), sh3, sc3, g3,
                 ffn2_w1[l].astype(BF16), ffn2_w3[l].astype(BF16), ffn2_w2[l].astype(BF16))
    return h
```
